```python
import math
import jax
import jax.numpy as jnp
from jax import lax
import numpy as np

D_MODEL = 4096
BATCH = 4
SEQ = 2048
DEPTH = 2
DEC_BATCH = 128
DEC_SEQ = 8
PAST_LEN = 16384
PAGE_SIZE = 128

MLA_HEADS = 16
QK_NOPE = 128
QK_ROPE = 64
V_HEAD = 128
Q_LORA = 1024
KV_LORA = 512
ROPE_THETA = 10000.0
Q_BLOCK = 128
MLA_SCALE = (QK_NOPE + QK_ROPE) ** -0.5
GDN_HEADS = 16
GDN_DK = 128
GDN_DV = 128
GDN_CONV = 4
GDN_CHUNK = 64
CONF_C = 2048
CONF_K = 31
D_FF = 8192
N_EXPERTS = 8
TOP_K = 2
D_EXPERT = 4096
N_BRANCH = 3
NORM_EPS = 1e-6

MLA_WIDTH = MLA_HEADS * V_HEAD
GDN_KEY = GDN_HEADS * GDN_DK
GDN_VAL = GDN_HEADS * GDN_DV
GDN_QKV = 2 * GDN_KEY + GDN_VAL
IN_SIZES = (Q_LORA, KV_LORA, QK_ROPE, GDN_QKV, GDN_VAL, GDN_HEADS, GDN_HEADS, 2 * CONF_C, N_BRANCH * D_MODEL)
IN_TOTAL = sum(IN_SIZES)
N_DENSE = (DEPTH + 1) // 2
N_MOE = DEPTH // 2

kernel_name = 'hybrid_mla_gdn_conformer_decode_step'

F32 = jnp.float32
MIXER_PARAM_NAMES = ('w_in', 'q_norm', 'kv_norm', 'w_q_b', 'w_uk', 'w_uv', 'gdn_conv_w', 'gdn_A_log', 'gdn_dt_bias',
                     'gdn_o_norm', 'conf_dw_w', 'conf_dw_b', 'conf_ln_g', 'conf_ln_b', 'w_o_mla', 'w_o_gdn', 'w_o_conf', 'w_out')


def rms_norm(x, g):
    xf = x.astype(F32)
    xf = xf * lax.rsqrt(jnp.mean(xf * xf, axis=-1, keepdims=True) + NORM_EPS)
    return (xf * g.astype(F32)).astype(x.dtype)


def layer_norm(x, g, b):
    xf = x.astype(F32)
    xc = xf - jnp.mean(xf, axis=-1, keepdims=True)
    var = jnp.mean(xc * xc, axis=-1, keepdims=True)
    return (xc * lax.rsqrt(var + NORM_EPS) * g.astype(F32) + b.astype(F32)).astype(x.dtype)


def l2_normalize(x):
    xf = x.astype(F32)
    return xf * lax.rsqrt(jnp.sum(xf * xf, axis=-1, keepdims=True) + NORM_EPS)


def rope(x, pos):
    half = QK_ROPE // 2
    inv_freq = jnp.power(ROPE_THETA, -jnp.arange(half, dtype=F32) / half)
    ang = pos[:, None] * inv_freq[None, :]
    ang = ang.reshape(ang.shape[:1] + (1,) * (x.ndim - 3) + (half,))
    cos, sin = jnp.cos(ang), jnp.sin(ang)
    xf = x.astype(F32)
    x1, x2 = xf[..., :half], xf[..., half:]
    return jnp.concatenate([x1 * cos - x2 * sin, x2 * cos + x1 * sin], axis=-1).astype(x.dtype)


def causal_depthwise_conv(x, buf, w):
    xx = jnp.concatenate([buf.astype(x.dtype), x], axis=1)
    y = lax.conv_general_dilated(xx, w.astype(x.dtype)[:, None, :], window_strides=(1,), padding='VALID',
                                 dimension_numbers=('NWC', 'WIO', 'NWC'), feature_group_count=x.shape[-1])
    return y, xx[:, x.shape[1]:]


def swiglu(x, w1, w3, w2):
    return (jax.nn.silu(x @ w1) * (x @ w3)) @ w2


def moe_swiglu(x, router, w1, w3, w2):
    logits = jnp.einsum('btd,de->bte', x, router).astype(F32)
    probs = jax.nn.softmax(logits, axis=-1)
    top_p, top_i = lax.top_k(probs, TOP_K)
    top_p = top_p / jnp.sum(top_p, axis=-1, keepdims=True)
    gate = jnp.einsum('btk,btke->bte', top_p, jax.nn.one_hot(top_i, N_EXPERTS, dtype=F32)).astype(x.dtype)
    y = jnp.zeros_like(x)
    for e in range(N_EXPERTS):
        y = y + gate[..., e:e + 1] * swiglu(x, w1[e], w3[e], w2[e])
    return y


def mla_project(c_q, c_kv, k_pe, pos, q_norm, kv_norm, w_q_b, w_uk):
    c_q = rms_norm(c_q, q_norm)
    q = jnp.einsum('btr,rhe->bthe', c_q, w_q_b)
    q_nope, q_pe = q[..., :QK_NOPE], q[..., QK_NOPE:]
    q_lat = jnp.einsum('bthn,chn->bthc', q_nope, w_uk)
    return q_lat, rope(q_pe, pos), rms_norm(c_kv, kv_norm), rope(k_pe, pos)


def mla_scores(q_lat, q_pe, c_kv, k_pe):
    s = (jnp.einsum('bqhc,bkc->bhqk', q_lat, c_kv, preferred_element_type=F32)
         + jnp.einsum('bqhr,bkr->bhqk', q_pe, k_pe, preferred_element_type=F32))
    return s * MLA_SCALE


def mla_prompt_attend(q_lat, q_pe, c_kv, k_pe):
    B, T = q_lat.shape[:2]
    nb = T // Q_BLOCK
    key_pos = jnp.arange(T)
    ql = jnp.moveaxis(q_lat.reshape(B, nb, Q_BLOCK, MLA_HEADS, KV_LORA), 1, 0)
    qp = jnp.moveaxis(q_pe.reshape(B, nb, Q_BLOCK, MLA_HEADS, QK_ROPE), 1, 0)
    c_kv_f = c_kv.astype(F32)

    def block(args):
        i, ql_b, qp_b = args
        s = mla_scores(ql_b, qp_b, c_kv, k_pe)
        q_pos = i * Q_BLOCK + jnp.arange(Q_BLOCK)
        s = jnp.where(key_pos[None, :] <= q_pos[:, None], s, -jnp.inf)
        p = jax.nn.softmax(s, axis=-1)
        return jnp.einsum('bhqk,bkc->bqhc', p, c_kv_f)

    o = lax.map(block, (jnp.arange(nb), ql, qp))
    return jnp.moveaxis(o, 0, 1).reshape(B, T, MLA_HEADS, KV_LORA).astype(q_lat.dtype)


def mla_sample_attend(q_lat, q_pe, c_kv, k_pe, cache_kv_latent, cache_k_rope, page_table, layer):
    Bd, T = q_lat.shape[:2]

    def merge(carry, s, ckv):
        m, l, acc = carry
        m_new = jnp.maximum(m, jnp.max(s, axis=-1))
        p = jnp.exp(s - m_new[..., None])
        corr = jnp.exp(m - m_new)
        return (m_new, l * corr + jnp.sum(p, axis=-1),
                acc * corr[..., None] + jnp.einsum('bhqk,bkc->bhqc', p, ckv.astype(F32)))

    def step(carry, pages):
        ckv = cache_kv_latent[layer, pages]
        kpe = cache_k_rope[layer, pages]
        return merge(carry, mla_scores(q_lat, q_pe, ckv, kpe), ckv), None

    init = (jnp.full((Bd, MLA_HEADS, T), -jnp.inf, F32), jnp.zeros((Bd, MLA_HEADS, T), F32),
            jnp.zeros((Bd, MLA_HEADS, T, KV_LORA), F32))
    carry, _ = lax.scan(step, init, page_table.T)
    causal = jnp.tril(jnp.ones((T, T), dtype=bool))
    s_new = jnp.where(causal, mla_scores(q_lat, q_pe, c_kv, k_pe), -jnp.inf)
    m, l, acc = merge(carry, s_new, c_kv)
    o = acc / l[..., None]
    return jnp.moveaxis(o, 1, 2).astype(q_lat.dtype)


def gated_delta_rule(q, k, v, g, beta, S0):
    B, T, H, DK = q.shape
    DV = v.shape[-1]
    C = math.gcd(T, GDN_CHUNK)
    n = T // C

    def chunks(a):
        return jnp.moveaxis(a.reshape((B, n, C, H) + a.shape[3:]), (1, 3), (0, 2))

    idx = jnp.arange(C)
    incl = idx[:, None] >= idx[None, :]
    strict = idx[:, None] > idx[None, :]
    eye = jnp.eye(C, dtype=F32)

    def step(S, inp):
        qc, kc, vc, gc, bc = inp
        G = jnp.cumsum(gc, axis=-1)
        decay = jnp.exp(jnp.where(incl, G[..., :, None] - G[..., None, :], -jnp.inf))
        kk = jnp.einsum('bhik,bhjk->bhij', kc, kc)
        L = jnp.where(strict, bc[..., :, None] * decay * kk, 0.0)
        eG = jnp.exp(G)[..., None]
        rhs = bc[..., None] * (vc - eG * jnp.einsum('bhck,bhkv->bhcv', kc, S))
        U = lax.linalg.triangular_solve(L + eye, rhs, left_side=True, lower=True)
        qk = jnp.einsum('bhik,bhjk->bhij', qc, kc) * decay
        o = eG * jnp.einsum('bhck,bhkv->bhcv', qc, S) + jnp.einsum('bhij,bhjv->bhiv', qk, U)
        G_last = G[..., -1:]
        S_new = (jnp.exp(G_last)[..., None] * S
                 + jnp.einsum('bhck,bhcv->bhkv', kc * jnp.exp(G_last - G)[..., None], U))
        return S_new, o

    S, o = lax.scan(step, S0, (chunks(q), chunks(k), chunks(v), chunks(g), chunks(beta)))
    o = jnp.moveaxis(o, (0, 2), (1, 3)).reshape(B, T, H, DV)
    return o, S


def gdn_branch(qkv, z, beta_raw, alpha_raw, S0, conv_buf, conv_w, A_log, dt_bias, o_norm):
    B, T, _ = qkv.shape
    qkv_c, new_buf = causal_depthwise_conv(qkv, conv_buf, conv_w)
    qkv_c = jax.nn.silu(qkv_c)
    q = l2_normalize(qkv_c[..., :GDN_KEY].reshape(B, T, GDN_HEADS, GDN_DK)) * (GDN_DK ** -0.5)
    k = l2_normalize(qkv_c[..., GDN_KEY:2 * GDN_KEY].reshape(B, T, GDN_HEADS, GDN_DK))
    v = qkv_c[..., 2 * GDN_KEY:].reshape(B, T, GDN_HEADS, GDN_DV).astype(F32)
    beta = jax.nn.sigmoid(beta_raw.astype(F32))
    g = -jnp.exp(A_log.astype(F32)) * jax.nn.softplus(alpha_raw.astype(F32) + dt_bias.astype(F32))
    o, S = gated_delta_rule(q, k, v, g, beta, S0.astype(F32))
    o = rms_norm(o, o_norm) * jax.nn.silu(z.astype(F32).reshape(B, T, GDN_HEADS, GDN_DV))
    return o.reshape(B, T, GDN_VAL).astype(qkv.dtype), S.astype(S0.dtype), new_buf


def conformer_branch(glu_in, buf, dw_w, dw_b, ln_g, ln_b):
    a, b = jnp.split(glu_in, 2, axis=-1)
    u = a * jax.nn.sigmoid(b)
    h, new_buf = causal_depthwise_conv(u, buf, dw_w)
    h = layer_norm(h + dw_b.astype(h.dtype), ln_g, ln_b)
    return jax.nn.silu(h), new_buf


def mixer(xn, pos, attend, gdn_S0, gdn_buf, conf_buf, p):
    B, T, _ = xn.shape
    proj = xn @ p['w_in']
    offs = np.cumsum(IN_SIZES)[:-1].tolist()
    c_q, c_kv, k_pe, qkv, z, beta_raw, alpha_raw, glu_in, gate_logits = jnp.split(proj, offs, axis=-1)
    q_lat, q_pe, c_kv, k_pe = mla_project(c_q, c_kv, k_pe, pos, p['q_norm'], p['kv_norm'], p['w_q_b'], p['w_uk'])
    o_lat = attend(q_lat, q_pe, c_kv, k_pe)
    o_mla = jnp.einsum('bthc,chv->bthv', o_lat, p['w_uv']).reshape(B, T, MLA_WIDTH)
    o_gdn, S_new, gdn_buf_new = gdn_branch(qkv, z, beta_raw, alpha_raw, gdn_S0, gdn_buf, p['gdn_conv_w'],
                                           p['gdn_A_log'], p['gdn_dt_bias'], p['gdn_o_norm'])
    h_conf, conf_buf_new = conformer_branch(glu_in, conf_buf, p['conf_dw_w'], p['conf_dw_b'], p['conf_ln_g'], p['conf_ln_b'])
    gates = jax.nn.sigmoid(gate_logits.astype(F32)).reshape(B, T, N_BRANCH, D_MODEL).astype(xn.dtype)
    merged = (gates[:, :, 0] * (o_mla @ p['w_o_mla']) + gates[:, :, 1] * (o_gdn @ p['w_o_gdn'])
              + gates[:, :, 2] * (h_conf @ p['w_o_conf']))
    return merged @ p['w_out'], (c_kv, k_pe, S_new, gdn_buf_new, conf_buf_new)


def trunk(x, pos, attend_for_layer, gdn_S0, gdn_buf0, conf_buf0, P):
    new_states = []
    for l in range(DEPTH):
        lp = {name: P[name][l] for name in MIXER_PARAM_NAMES}
        m, st = mixer(rms_norm(x, P['attn_norm'][l]), pos, attend_for_layer(l), gdn_S0[l], gdn_buf0[l], conf_buf0[l], lp)
        x = x + m
        xn = rms_norm(x, P['ffn_norm'][l])
        i = l // 2
        if l % 2 == 0:
            x = x + swiglu(xn, P['ffn_w1'][i], P['ffn_w3'][i], P['ffn_w2'][i])
        else:
            x = x + moe_swiglu(xn, P['moe_router'][i], P['moe_w1'][i], P['moe_w3'][i], P['moe_w2'][i])
        new_states.append(st)
    y = rms_norm(x, P['final_norm'])
    return y, [jnp.stack([st[j] for st in new_states]) for j in range(5)]


def setup_inputs(seed: int = 0) -> dict:
    key = jax.random.key(seed)
    keys = jax.random.split(key, 64)
    counter = [0]

    def nk():
        counter[0] += 1
        return keys[counter[0] - 1]

    def nrm(shape, scale):
        return jax.random.normal(nk(), shape, F32) * scale

    def gain(shape):
        return 1.0 + 0.05 * jax.random.normal(nk(), shape, F32)

    n_pages = PAST_LEN // PAGE_SIZE
    n_used = DEC_BATCH * n_pages
    n_pool = n_used + max(1, n_used // 4)
    dt = jnp.exp(jax.random.uniform(nk(), (DEPTH, GDN_HEADS), F32, math.log(1e-3), math.log(1e-1)))
    return {
        'x_prompt': nrm((BATCH, SEQ, D_MODEL), 1.0),
        'x_sample': nrm((DEC_BATCH, DEC_SEQ, D_MODEL), 1.0),
        'cache_kv_latent': nrm((DEPTH, n_pool, PAGE_SIZE, KV_LORA), 1.0),
        'cache_k_rope': nrm((DEPTH, n_pool, PAGE_SIZE, QK_ROPE), 1.0),
        'state_gdn': nrm((DEPTH, DEC_BATCH, GDN_HEADS, GDN_DK, GDN_DV), 0.1),
        'state_gdn_conv': nrm((DEPTH, DEC_BATCH, GDN_CONV - 1, GDN_QKV), 1.0),
        'state_conf_conv': nrm((DEPTH, DEC_BATCH, CONF_K - 1, CONF_C), 0.7),
        'page_table': jax.random.permutation(nk(), n_pool)[:n_used].reshape(DEC_BATCH, n_pages).astype(jnp.int32),
        'attn_norm': gain((DEPTH, D_MODEL)),
        'w_in': nrm((DEPTH, D_MODEL, IN_TOTAL), D_MODEL ** -0.5),
        'q_norm': gain((DEPTH, Q_LORA)),
        'kv_norm': gain((DEPTH, KV_LORA)),
        'w_q_b': nrm((DEPTH, Q_LORA, MLA_HEADS, QK_NOPE + QK_ROPE), Q_LORA ** -0.5),
        'w_uk': nrm((DEPTH, KV_LORA, MLA_HEADS, QK_NOPE), KV_LORA ** -0.5),
        'w_uv': nrm((DEPTH, KV_LORA, MLA_HEADS, V_HEAD), KV_LORA ** -0.5),
        'gdn_conv_w': nrm((DEPTH, GDN_CONV, GDN_QKV), GDN_CONV ** -0.5),
        'gdn_A_log': jnp.log(jax.random.uniform(nk(), (DEPTH, GDN_HEADS), F32, 1.0, 16.0)),
        'gdn_dt_bias': dt + jnp.log(-jnp.expm1(-dt)),
        'gdn_o_norm': gain((DEPTH, GDN_DV)),
        'conf_dw_w': nrm((DEPTH, CONF_K, CONF_C), CONF_K ** -0.5),
        'conf_dw_b': nrm((DEPTH, CONF_C), 0.02),
        'conf_ln_g': gain((DEPTH, CONF_C)),
        'conf_ln_b': nrm((DEPTH, CONF_C), 0.02),
        'w_o_mla': nrm((DEPTH, MLA_WIDTH, D_MODEL), MLA_WIDTH ** -0.5),
        'w_o_gdn': nrm((DEPTH, GDN_VAL, D_MODEL), GDN_VAL ** -0.5),
        'w_o_conf': nrm((DEPTH, CONF_C, D_MODEL), CONF_C ** -0.5),
        'w_out': nrm((DEPTH, D_MODEL, D_MODEL), D_MODEL ** -0.5),
        'ffn_norm': gain((DEPTH, D_MODEL)),
        'ffn_w1': nrm((N_DENSE, D_MODEL, D_FF), D_MODEL ** -0.5),
        'ffn_w3': nrm((N_DENSE, D_MODEL, D_FF), D_MODEL ** -0.5),
        'ffn_w2': nrm((N_DENSE, D_FF, D_MODEL), D_FF ** -0.5),
        'moe_router': nrm((N_MOE, D_MODEL, N_EXPERTS), D_MODEL ** -0.5),
        'moe_w1': nrm((N_MOE, N_EXPERTS, D_MODEL, D_EXPERT), D_MODEL ** -0.5),
        'moe_w3': nrm((N_MOE, N_EXPERTS, D_MODEL, D_EXPERT), D_MODEL ** -0.5),
        'moe_w2': nrm((N_MOE, N_EXPERTS, D_EXPERT, D_MODEL), D_EXPERT ** -0.5),
        'final_norm': gain((D_MODEL,)),
    }


def reference(x_prompt, x_sample, cache_kv_latent, cache_k_rope, state_gdn, state_gdn_conv, state_conf_conv, page_table,
              attn_norm, w_in, q_norm, kv_norm, w_q_b, w_uk, w_uv, gdn_conv_w, gdn_A_log, gdn_dt_bias, gdn_o_norm,
              conf_dw_w, conf_dw_b, conf_ln_g, conf_ln_b, w_o_mla, w_o_gdn, w_o_conf, w_out, ffn_norm,
              ffn_w1, ffn_w3, ffn_w2, moe_router, moe_w1, moe_w3, moe_w2, final_norm):
    P = {'attn_norm': attn_norm, 'w_in': w_in, 'q_norm': q_norm, 'kv_norm': kv_norm, 'w_q_b': w_q_b, 'w_uk': w_uk,
         'w_uv': w_uv, 'gdn_conv_w': gdn_conv_w, 'gdn_A_log': gdn_A_log, 'gdn_dt_bias': gdn_dt_bias,
         'gdn_o_norm': gdn_o_norm, 'conf_dw_w': conf_dw_w, 'conf_dw_b': conf_dw_b, 'conf_ln_g': conf_ln_g,
         'conf_ln_b': conf_ln_b, 'w_o_mla': w_o_mla, 'w_o_gdn': w_o_gdn, 'w_o_conf': w_o_conf, 'w_out': w_out,
         'ffn_norm': ffn_norm, 'ffn_w1': ffn_w1, 'ffn_w3': ffn_w3, 'ffn_w2': ffn_w2, 'moe_router': moe_router,
         'moe_w1': moe_w1, 'moe_w3': moe_w3, 'moe_w2': moe_w2, 'final_norm': final_norm}
    B, T = x_prompt.shape[:2]
    Td = x_sample.shape[1]
    pos_prompt = jnp.arange(T, dtype=F32)
    pos_sample = PAST_LEN + jnp.arange(Td, dtype=F32)
    zero_S = jnp.zeros((DEPTH, B, GDN_HEADS, GDN_DK, GDN_DV), x_prompt.dtype)
    zero_gconv = jnp.zeros((DEPTH, B, GDN_CONV - 1, GDN_QKV), x_prompt.dtype)
    zero_cconv = jnp.zeros((DEPTH, B, CONF_K - 1, CONF_C), x_prompt.dtype)
    y_prompt, st_p = trunk(x_prompt, pos_prompt, lambda l: mla_prompt_attend, zero_S, zero_gconv, zero_cconv, P)

    def sample_attend(l):
        def attend(q_lat, q_pe, c_kv, k_pe):
            return mla_sample_attend(q_lat, q_pe, c_kv, k_pe, cache_kv_latent, cache_k_rope, page_table, l)
        return attend

    y_sample, st_s = trunk(x_sample, pos_sample, sample_attend, state_gdn, state_gdn_conv, state_conf_conv, P)
    return (y_prompt, y_sample, st_p[0], st_p[1], st_p[2], st_p[3], st_p[4], st_s[0], st_s[1], st_s[2], st_s[3], st_s[4])
```

```python
import functools
import math

import jax
import jax.numpy as jnp
from jax import lax
from jax.experimental import pallas as pl
from jax.experimental.pallas import tpu as pltpu

F32 = jnp.float32
BF16 = jnp.bfloat16
HIGHEST = lax.Precision.HIGHEST
NORM_EPS = 1e-6
ROPE_THETA = 10000.0
VMEM_LIMIT_BYTES = 56 * 1024 * 1024
LANES = 128
TRI_BLOCK = 16


def _cparams(*sem):
    return pltpu.CompilerParams(dimension_semantics=sem, vmem_limit_bytes=VMEM_LIMIT_BYTES)


def _dot(a, b, precision=None):
    return jnp.dot(a, b, preferred_element_type=F32, precision=precision)


def _dot_nt(a, b, precision=None):
    return lax.dot_general(a, b, (((1,), (1,)), ((), ())), preferred_element_type=F32, precision=precision)


def _dot_tn(a, b, precision=None):
    return lax.dot_general(a, b, (((0,), (0,)), ((), ())), preferred_element_type=F32, precision=precision)


def _sigmoid(x):
    return 1.0 / (1.0 + jnp.exp(-x))


def _silu(x):
    return x * _sigmoid(x)


def _rmsnorm_kernel(x_ref, g_ref, o_ref):
    x = x_ref[...]
    ms = jnp.mean(x * x, axis=-1, keepdims=True)
    o_ref[...] = (x * lax.rsqrt(ms + NORM_EPS) * g_ref[...]).astype(o_ref.dtype)


def rmsnorm(x, g, out_dtype, *, row0=0, rows=None, tm=256):
    M, D = x.shape
    rows = M - row0 if rows is None else rows
    tm = min(tm, rows)
    b0 = row0 // tm
    return pl.pallas_call(
        _rmsnorm_kernel,
        grid=(rows // tm,),
        in_specs=[pl.BlockSpec((tm, D), lambda i: (b0 + i, 0)), pl.BlockSpec((1, D), lambda i: (0, 0))],
        out_specs=pl.BlockSpec((tm, D), lambda i: (i, 0)),
        out_shape=jax.ShapeDtypeStruct((rows, D), out_dtype),
        compiler_params=_cparams("parallel"),
        name="rmsnorm",
    )(x, g.reshape(1, D))


def fused_matmul(a_ops, w_ops, amap, e_ops, epilogue, out_shape, out_spec, grid, acc_shape, name):
    na, nd, ne = len(a_ops), len(w_ops), len(e_ops)
    nk = grid[2]

    def kernel(*refs):
        a_refs = refs[:na]
        w_refs = refs[na:na + nd]
        e_refs = refs[na + nd:na + nd + ne]
        o_ref = refs[na + nd + ne]
        acc_refs = refs[na + nd + ne + 1:]
        dots = [_dot(a_refs[amap[d]][...], w_refs[d][...]) for d in range(nd)]
        if nk == 1:
            o_ref[...] = epilogue(dots, [e[...] for e in e_refs]).astype(o_ref.dtype)
        else:
            k = pl.program_id(2)

            @pl.when(k == 0)
            def _():
                for d in range(nd):
                    acc_refs[d][...] = dots[d]

            @pl.when(k > 0)
            def _():
                for d in range(nd):
                    acc_refs[d][...] += dots[d]

            @pl.when(k == nk - 1)
            def _():
                o_ref[...] = epilogue([acc[...] for acc in acc_refs], [e[...] for e in e_refs]).astype(o_ref.dtype)

    scratch = [pltpu.VMEM(acc_shape, F32) for _ in range(nd)] if nk > 1 else []
    ops = a_ops + w_ops + e_ops
    return pl.pallas_call(
        kernel,
        grid=grid,
        in_specs=[s for _, s in ops],
        out_specs=out_spec,
        out_shape=out_shape,
        scratch_shapes=scratch,
        compiler_params=_cparams("parallel", "parallel", "arbitrary"),
        name=name,
    )(*[a for a, _ in ops])


def matmul(a, w, out_dtype, *, tm, tn, tk=None, epilogue=None, extras=(), n_out=None, w_col_blocks=(0,), name="matmul"):
    M, K = a.shape
    n_out = w.shape[1] if n_out is None else n_out
    tk = K if tk is None else tk
    grid = (M // tm, n_out // tn, K // tk)
    a_ops = [(a, pl.BlockSpec((tm, tk), lambda i, j, k: (i, k)))]
    w_ops = [(w, pl.BlockSpec((tk, tn), functools.partial(lambda i, j, k, off: (k, j + off), off=off)))
             for off in w_col_blocks]
    if epilogue is None:
        epilogue = lambda dots, ex: dots[0]
    return fused_matmul(a_ops, w_ops, [0] * len(w_ops), list(extras), epilogue,
                        jax.ShapeDtypeStruct((M, n_out), out_dtype),
                        pl.BlockSpec((tm, tn), lambda i, j, k: (i, j)), grid, (tm, tn), name)


def _tile_spec(tm, tn, col_off=0):
    return pl.BlockSpec((tm, tn), lambda i, j, k: (i, j + col_off))


def _rope_rows(x, cosf, sinf):
    half = x.shape[-1] // 2
    swapped = jnp.concatenate([x[:, half:], x[:, :half]], axis=-1)
    return x * cosf + swapped * sinf


def _mla_q_kernel(cq_ref, qn_ref, wqn_ref, wqp_ref, wuk_ref, cos_ref, sin_ref, ql_ref, qp_ref, cqn_sc, *, scale):
    @pl.when(pl.program_id(1) == 0)
    def _():
        x = cq_ref[...]
        ms = jnp.mean(x * x, axis=-1, keepdims=True)
        cqn_sc[...] = (x * lax.rsqrt(ms + NORM_EPS) * qn_ref[...]).astype(BF16)

    cqn = cqn_sc[...]
    q_nope = _dot(cqn, wqn_ref[...])
    q_pe = _dot(cqn, wqp_ref[...])
    q_lat = _dot(q_nope.astype(BF16), wuk_ref[...])
    ql_ref[...] = (q_lat * scale).astype(BF16)
    qp_ref[...] = (_rope_rows(q_pe, cos_ref[...], sin_ref[...]) * scale).astype(BF16)


def mla_q(y_small, q_norm, wqn, wqp, wukT, cosf, sinf, *, q_lora, scale, tm=512):
    M = y_small.shape[0]
    H, _, nope = wqn.shape
    rope_d = wqp.shape[2]
    kv_lora = wukT.shape[2]
    return pl.pallas_call(
        functools.partial(_mla_q_kernel, scale=scale),
        grid=(M // tm, H),
        in_specs=[
            pl.BlockSpec((tm, q_lora), lambda i, h: (i, 0)),
            pl.BlockSpec((1, q_lora), lambda i, h: (0, 0)),
            pl.BlockSpec((None, q_lora, nope), lambda i, h: (h, 0, 0)),
            pl.BlockSpec((None, q_lora, rope_d), lambda i, h: (h, 0, 0)),
            pl.BlockSpec((None, nope, kv_lora), lambda i, h: (h, 0, 0)),
            pl.BlockSpec((tm, rope_d), lambda i, h: (i, 0)),
            pl.BlockSpec((tm, rope_d), lambda i, h: (i, 0)),
        ],
        out_specs=[pl.BlockSpec((None, tm, kv_lora), lambda i, h: (h, i, 0)),
                   pl.BlockSpec((None, tm, rope_d), lambda i, h: (h, i, 0))],
        out_shape=[jax.ShapeDtypeStruct((H, M, kv_lora), BF16), jax.ShapeDtypeStruct((H, M, rope_d), BF16)],
        scratch_shapes=[pltpu.VMEM((tm, q_lora), BF16)],
        compiler_params=_cparams("parallel", "arbitrary"),
        name="mla_q",
    )(y_small, q_norm.reshape(1, -1), wqn, wqp, wukT, cosf, sinf)


def _mla_kv_kernel(ckv_ref, kpe_ref, g_ref, cos_ref, sin_ref, ckv_o, kpe_o, ckvb_o, kpeb_o):
    x = ckv_ref[...]
    ms = jnp.mean(x * x, axis=-1, keepdims=True)
    c = x * lax.rsqrt(ms + NORM_EPS) * g_ref[...]
    rope_d = kpe_o.shape[-1]
    kr = _rope_rows(kpe_ref[...][:, :rope_d], cos_ref[...], sin_ref[...])
    ckv_o[...] = c
    kpe_o[...] = kr
    ckvb_o[...] = c.astype(BF16)
    kpeb_o[...] = kr.astype(BF16)


def mla_kv(y_small, kv_norm, cosf, sinf, *, q_lora, kv_lora, tm=512):
    M = y_small.shape[0]
    rope_d = cosf.shape[1]
    row = lambda i: (i, 0)
    return pl.pallas_call(
        _mla_kv_kernel,
        grid=(M // tm,),
        in_specs=[
            pl.BlockSpec((tm, kv_lora), lambda i: (i, q_lora // kv_lora)),
            pl.BlockSpec((tm, LANES), lambda i: (i, (q_lora + kv_lora) // LANES)),
            pl.BlockSpec((1, kv_lora), lambda i: (0, 0)),
            pl.BlockSpec((tm, rope_d), row),
            pl.BlockSpec((tm, rope_d), row),
        ],
        out_specs=[pl.BlockSpec((tm, kv_lora), row), pl.BlockSpec((tm, rope_d), row),
                   pl.BlockSpec((tm, kv_lora), row), pl.BlockSpec((tm, rope_d), row)],
        out_shape=[jax.ShapeDtypeStruct((M, kv_lora), F32), jax.ShapeDtypeStruct((M, rope_d), F32),
                   jax.ShapeDtypeStruct((M, kv_lora), BF16), jax.ShapeDtypeStruct((M, rope_d), BF16)],
        compiler_params=_cparams("parallel"),
        name="mla_kv",
    )(y_small, y_small, kv_norm.reshape(1, -1), cosf, sinf)


def _softmax_update(s, v_bf, m_sc, l_sc, acc_sc):
    m_prev = m_sc[...]
    m_new = jnp.maximum(m_prev, jnp.max(s, axis=-1, keepdims=True))
    p = jnp.exp(s - m_new)
    corr = jnp.exp(m_prev - m_new)
    l_sc[...] = l_sc[...] * corr + jnp.sum(p, axis=-1, keepdims=True)
    acc_sc[...] = acc_sc[...] * corr + _dot(p.astype(BF16), v_bf)
    m_sc[...] = m_new


def _prompt_attn_kernel(ql_ref, qp_ref, k_ref, kr_ref, o_ref, m_sc, l_sc, acc_sc, *, tq, tk):
    i = pl.program_id(1)
    j = pl.program_id(2)
    H = ql_ref.shape[0]
    R = H * tq
    j_last = (i * tq + tq - 1) // tk

    @pl.when(j == 0)
    def _():
        m_sc[...] = jnp.full(m_sc.shape, -jnp.inf, F32)
        l_sc[...] = jnp.zeros(l_sc.shape, F32)
        acc_sc[...] = jnp.zeros(acc_sc.shape, F32)

    @pl.when(j <= j_last)
    def _():
        ql = ql_ref[...].reshape(R, ql_ref.shape[-1])
        qp = qp_ref[...].reshape(R, qp_ref.shape[-1])
        k = k_ref[...]
        s = _dot_nt(ql, k) + _dot_nt(qp, kr_ref[...])
        q_pos = i * tq + lax.broadcasted_iota(jnp.int32, (R, tk), 0) % tq
        k_pos = j * tk + lax.broadcasted_iota(jnp.int32, (R, tk), 1)
        s = jnp.where(k_pos <= q_pos, s, -jnp.inf)
        _softmax_update(s, k, m_sc, l_sc, acc_sc)

    @pl.when(j == j_last)
    def _():
        o = acc_sc[...] / l_sc[...]
        o_ref[...] = o.astype(BF16).reshape(o_ref.shape)


def mla_prompt_attention(q_lat, q_pe, ckv_bf, kpe_bf, *, B, T, m_total, tq=128, tk=512):
    H, _, C = q_lat.shape
    rope_d = q_pe.shape[2]
    nq, nk = T // tq, T // tk

    def kmap(b, i, j):
        return (b * nk + jnp.minimum(j, (i * tq + tq - 1) // tk), 0)

    qmap = lambda b, i, j: (0, b * nq + i, 0)
    return pl.pallas_call(
        functools.partial(_prompt_attn_kernel, tq=tq, tk=tk),
        grid=(B, nq, nk),
        in_specs=[pl.BlockSpec((H, tq, C), qmap), pl.BlockSpec((H, tq, rope_d), qmap),
                  pl.BlockSpec((tk, C), kmap), pl.BlockSpec((tk, rope_d), kmap)],
        out_specs=pl.BlockSpec((H, tq, C), qmap),
        out_shape=jax.ShapeDtypeStruct((H, m_total, C), BF16),
        scratch_shapes=[pltpu.VMEM((H * tq, 1), F32), pltpu.VMEM((H * tq, 1), F32), pltpu.VMEM((H * tq, C), F32)],
        compiler_params=_cparams("parallel", "parallel", "arbitrary"),
        name="mla_prompt_attn",
    )(q_lat, q_pe, ckv_bf, kpe_bf)


def _sample_attn_kernel(pt_ref, ql_ref, qp_ref, *refs, pps, page, t_new):
    kv_refs = refs[:pps]
    kr_refs = refs[pps:2 * pps]
    nkv_ref, nkr_ref, o_ref, kbuf, rbuf, m_sc, l_sc, acc_sc = refs[2 * pps:]
    s_idx = pl.program_id(1)

    @pl.when(s_idx == 0)
    def _():
        m_sc[...] = jnp.full(m_sc.shape, -jnp.inf, F32)
        l_sc[...] = jnp.zeros(l_sc.shape, F32)
        acc_sc[...] = jnp.zeros(acc_sc.shape, F32)

    for j in range(pps):
        kbuf[j * page:(j + 1) * page, :] = kv_refs[j][...].astype(BF16)
        rbuf[j * page:(j + 1) * page, :] = kr_refs[j][...].astype(BF16)
    ql = ql_ref[...]
    qp = qp_ref[...]
    kb = kbuf[...]
    s = _dot_nt(ql, kb) + _dot_nt(qp, rbuf[...])
    _softmax_update(s, kb, m_sc, l_sc, acc_sc)

    @pl.when(s_idx == pl.num_programs(1) - 1)
    def _():
        nk = nkv_ref[...].astype(BF16)
        R = ql.shape[0]
        s_new = _dot_nt(ql, nk) + _dot_nt(qp, nkr_ref[...].astype(BF16))
        q_t = lax.broadcasted_iota(jnp.int32, (R, t_new), 0) % t_new
        k_t = lax.broadcasted_iota(jnp.int32, (R, t_new), 1)
        s_new = jnp.where(k_t <= q_t, s_new, -jnp.inf)
        _softmax_update(s_new, nk, m_sc, l_sc, acc_sc)
        o_ref[...] = (acc_sc[...] / l_sc[...]).astype(BF16)


def mla_sample_attention(q_lat_s, q_pe_s, cache_kv, cache_kr, page_table, ckv_new, kpe_new, *, layer, new_row0, pps=16):
    Bd, R, C = q_lat_s.shape
    rope_d = q_pe_s.shape[2]
    n_pages = page_table.shape[1]
    page = cache_kv.shape[2]
    t_new = 8
    assert n_pages % pps == 0 and new_row0 % t_new == 0
    pt_flat = page_table.reshape(-1)

    def page_map(b, s, pt, *, j):
        return (layer, pt[b * n_pages + s * pps + j], 0, 0)

    qmap = lambda b, s, pt: (b, 0, 0)
    newmap = lambda b, s, pt: (new_row0 // t_new + b, 0)
    in_specs = [pl.BlockSpec((None, R, C), qmap), pl.BlockSpec((None, R, rope_d), qmap)]
    in_specs += [pl.BlockSpec((None, None, page, C), functools.partial(page_map, j=j)) for j in range(pps)]
    in_specs += [pl.BlockSpec((None, None, page, rope_d), functools.partial(page_map, j=j)) for j in range(pps)]
    in_specs += [pl.BlockSpec((t_new, C), newmap), pl.BlockSpec((t_new, rope_d), newmap)]
    grid_spec = pltpu.PrefetchScalarGridSpec(
        num_scalar_prefetch=1,
        grid=(Bd, n_pages // pps),
        in_specs=in_specs,
        out_specs=pl.BlockSpec((None, R, C), qmap),
        scratch_shapes=[pltpu.VMEM((pps * page, C), BF16), pltpu.VMEM((pps * page, rope_d), BF16),
                        pltpu.VMEM((R, 1), F32), pltpu.VMEM((R, 1), F32), pltpu.VMEM((R, C), F32)],
    )
    return pl.pallas_call(
        functools.partial(_sample_attn_kernel, pps=pps, page=page, t_new=t_new),
        grid_spec=grid_spec,
        out_shape=jax.ShapeDtypeStruct((Bd, R, C), BF16),
        compiler_params=_cparams("parallel", "arbitrary"),
        name="mla_sample_attn",
    )(pt_flat, q_lat_s, q_pe_s, *([cache_kv] * pps), *([cache_kr] * pps), ckv_new, kpe_new)


def _gdn_prep_kernel(x_ref, halo_ref, w_ref, o_ref, *, taps, tiles_per_seq, q_scale):
    i = pl.program_id(0)
    kind = pl.program_id(1)
    halo_rows = halo_ref.shape[0]
    tt = x_ref.shape[0]
    keep = (i % tiles_per_seq != 0).astype(F32)
    xx = jnp.concatenate([halo_ref[...] * keep, x_ref[...]], axis=0)
    w = w_ref[...]
    y = jnp.zeros(x_ref.shape, F32)
    for k in range(taps):
        off = halo_rows - (taps - 1) + k
        y = y + w[k:k + 1, :] * xx[off:off + tt, :]
    y = _silu(y)
    scale = jnp.where(kind == 0, q_scale, 1.0).astype(F32)
    for h in range(x_ref.shape[1] // LANES):
        yh = y[:, h * LANES:(h + 1) * LANES]
        nrm = yh * lax.rsqrt(jnp.sum(yh * yh, axis=-1, keepdims=True) + NORM_EPS) * scale
        o_ref[:, h * LANES:(h + 1) * LANES] = jnp.where(kind < 2, nrm, yh)


def gdn_prep(qkv, conv_w, *, row0, rows, tiles_per_seq, tt, key_width):
    taps, width = conv_w.shape
    halo = 8
    assert width == 3 * key_width and row0 % tt == 0 and rows % tt == 0
    b0, hb0 = row0 // tt, row0 // halo
    return pl.pallas_call(
        functools.partial(_gdn_prep_kernel, taps=taps, tiles_per_seq=tiles_per_seq, q_scale=LANES ** -0.5),
        grid=(rows // tt, 3),
        in_specs=[
            pl.BlockSpec((tt, key_width), lambda i, c: (b0 + i, c)),
            pl.BlockSpec((halo, key_width), lambda i, c: (jnp.maximum(hb0 + i * (tt // halo) - 1, 0), c)),
            pl.BlockSpec((taps, key_width), lambda i, c: (0, c)),
        ],
        out_specs=pl.BlockSpec((tt, key_width), lambda i, c: (i, c)),
        out_shape=jax.ShapeDtypeStruct((rows, width), F32),
        compiler_params=_cparams("parallel", "parallel"),
        name="gdn_prep",
    )(qkv, qkv, conv_w)


def _unit_lower_inverse(L, C, eye, blockdiag):
    blk = min(C, TRI_BLOCK)
    D = L if C <= TRI_BLOCK else jnp.where(blockdiag, L, 0.0)
    P = -D
    T = eye + P
    for _ in range(int(math.log2(blk)) - 1):
        P = _dot(P, P, HIGHEST)
        T = T + _dot(T, P, HIGHEST)
    if C > TRI_BLOCK:
        assert C // TRI_BLOCK <= 4
        Mx = _dot(T, L - D, HIGHEST)
        M2 = _dot(Mx, Mx, HIGHEST)
        ImM = eye - Mx
        R = ImM + _dot(ImM, M2, HIGHEST)
        T = _dot(R, T, HIGHEST)
    return T


def _gdn_scan_kernel(q_ref, k_ref, v_ref, br_ref, ar_ref, z_ref, alog_ref, dtb_ref, on_ref, s0_ref,
                     o_ref, s_out_ref, s_sc, *, hb, C):
    c = pl.program_id(2)
    hg = pl.program_id(1)

    @pl.when(c == 0)
    def _():
        s_sc[...] = s0_ref[...]

    row = lax.broadcasted_iota(jnp.int32, (C, C), 0)
    col = lax.broadcasted_iota(jnp.int32, (C, C), 1)
    incl = row >= col
    strict = row > col
    eye = (row == col).astype(F32)
    blockdiag = (row // TRI_BLOCK) == (col // TRI_BLOCK)

    beta_all = _sigmoid(br_ref[...])
    x = ar_ref[...] + dtb_ref[...]
    softplus = jnp.maximum(x, 0.0) + jnp.log(1.0 + jnp.exp(-jnp.abs(x)))
    g_all = -jnp.exp(alog_ref[...]) * softplus
    gc_all = _dot(incl.astype(F32), g_all, HIGHEST)
    gr_all = _dot_tn(g_all, (row <= col).astype(F32), HIGHEST)
    lane = lax.broadcasted_iota(jnp.int32, (C, LANES), 1)
    sub = lax.broadcasted_iota(jnp.int32, (LANES, C), 0)

    for hh in range(hb):
        head = hg * hb + hh
        sl = slice(hh * LANES, (hh + 1) * LANES)
        q, k, v = q_ref[:, sl], k_ref[:, sl], v_ref[:, sl]
        S = s_sc[hh]
        gcol = jnp.sum(jnp.where(lane == head, gc_all, 0.0), axis=1, keepdims=True)
        bcol = jnp.sum(jnp.where(lane == head, beta_all, 0.0), axis=1, keepdims=True)
        grow = jnp.sum(jnp.where(sub == head, gr_all, 0.0), axis=0, keepdims=True)
        decay = jnp.exp(jnp.where(incl, gcol - grow, -jnp.inf))
        kk = _dot_nt(k, k, HIGHEST)
        L = jnp.where(strict, bcol * decay * kk, 0.0)
        Tinv = _unit_lower_inverse(L, C, eye, blockdiag)
        eG = jnp.exp(gcol)
        rhs = bcol * (v - eG * _dot(k, S, HIGHEST))
        U = _dot(Tinv, rhs, HIGHEST)
        qk = _dot_nt(q, k, HIGHEST) * decay
        o = eG * _dot(q, S, HIGHEST) + _dot(qk, U, HIGHEST)
        g_last = gcol[C - 1:C, :]
        s_sc[hh] = jnp.exp(g_last) * S + _dot_tn(k * jnp.exp(g_last - gcol), U, HIGHEST)
        ms = jnp.mean(o * o, axis=-1, keepdims=True)
        on = o * lax.rsqrt(ms + NORM_EPS) * on_ref[...]
        o_ref[:, sl] = (on * _silu(z_ref[:, sl].astype(F32))).astype(o_ref.dtype)

    @pl.when(c == pl.num_programs(2) - 1)
    def _():
        s_out_ref[...] = s_sc[...]


def gdn_scan(qkvn, y_small, z, A_log, dt_bias, o_norm, S0, *, n_seq, T, C, qkv_blk0, qkv_blk_stride, tok_row0,
             beta_col, alpha_col, hb=4):
    H = S0.shape[1]
    DK, DV = S0.shape[2], S0.shape[3]
    assert DK == LANES and DV == LANES and H <= LANES and H % hb == 0
    nc = T // C
    ng = H // hb
    tb0 = tok_row0 // C

    def pad_lanes(a):
        return jnp.zeros((1, LANES), F32).at[0, :H].set(a.astype(F32))

    def qmap(s, g, c, *, grp):
        return (qkv_blk0 + qkv_blk_stride * (s * nc + c), grp * ng + g)

    tokmap = lambda s, g, c, col: (tb0 + s * nc + c, col)
    const = lambda s, g, c: (0, 0)
    smap = lambda s, g, c: (s, g, 0, 0)
    return pl.pallas_call(
        functools.partial(_gdn_scan_kernel, hb=hb, C=C),
        grid=(n_seq, ng, nc),
        in_specs=[
            pl.BlockSpec((C, hb * LANES), functools.partial(qmap, grp=0)),
            pl.BlockSpec((C, hb * LANES), functools.partial(qmap, grp=1)),
            pl.BlockSpec((C, hb * LANES), functools.partial(qmap, grp=2)),
            pl.BlockSpec((C, LANES), functools.partial(tokmap, col=beta_col)),
            pl.BlockSpec((C, LANES), functools.partial(tokmap, col=alpha_col)),
            pl.BlockSpec((C, hb * LANES), lambda s, g, c: (tb0 + s * nc + c, g)),
            pl.BlockSpec((1, LANES), const), pl.BlockSpec((1, LANES), const), pl.BlockSpec((1, LANES), const),
            pl.BlockSpec((None, hb, DK, DV), smap),
        ],
        out_specs=[pl.BlockSpec((C, hb * LANES), lambda s, g, c: (s * nc + c, g)),
                   pl.BlockSpec((None, hb, DK, DV), smap)],
        out_shape=[jax.ShapeDtypeStruct((n_seq * T, H * DV), F32), jax.ShapeDtypeStruct((n_seq, H, DK, DV), F32)],
        scratch_shapes=[pltpu.VMEM((hb, DK, DV), F32)],
        compiler_params=_cparams("parallel", "parallel", "arbitrary"),
        name="gdn_scan",
    )(qkvn, qkvn, qkvn, y_small, y_small, z, pad_lanes(A_log), pad_lanes(dt_bias), o_norm.reshape(1, DV).astype(F32), S0)


CONF_ROW_CHUNK = 32
CONF_LANE_CHUNK = 512


def _conformer_kernel(x_ref, halo_ref, w_ref, b_ref, g_ref, beta_ref, o_ref, xs, ys, *, taps, tiles_per_seq):
    i = pl.program_id(0)
    halo_rows = halo_ref.shape[0]
    tt, width = x_ref.shape
    keep = (i % tiles_per_seq != 0).astype(F32)
    xs[0:halo_rows, :] = halo_ref[...] * keep
    xs[halo_rows:halo_rows + tt, :] = x_ref[...]
    base = halo_rows - (taps - 1)

    for lc in range(width // CONF_LANE_CHUNK):
        ls = slice(lc * CONF_LANE_CHUNK, (lc + 1) * CONF_LANE_CHUNK)

        def body(r, carry):
            r0 = pl.multiple_of(r * CONF_ROW_CHUNK, CONF_ROW_CHUNK)
            acc = jnp.zeros((CONF_ROW_CHUNK, CONF_LANE_CHUNK), F32)
            win = xs[pl.ds(r0, CONF_ROW_CHUNK + halo_rows), ls]
            for k in range(taps):
                acc = acc + w_ref[k:k + 1, ls] * win[base + k:base + k + CONF_ROW_CHUNK, :]
            ys[pl.ds(r0, CONF_ROW_CHUNK), ls] = acc
            return carry

        lax.fori_loop(0, tt // CONF_ROW_CHUNK, body, 0)

    y = ys[...] + b_ref[...]
    mu = jnp.mean(y, axis=-1, keepdims=True)
    yc = y - mu
    var = jnp.mean(yc * yc, axis=-1, keepdims=True)
    h = yc * lax.rsqrt(var + NORM_EPS) * g_ref[...] + beta_ref[...]
    o_ref[...] = _silu(h).astype(o_ref.dtype)


def conformer_conv(u, dw_w, dw_b, ln_g, ln_b, *, row0, rows, tiles_per_seq, tt):
    taps, width = dw_w.shape
    halo = 32
    assert taps - 1 <= halo and row0 % tt == 0 and rows % tt == 0 and tt % halo == 0
    b0, hb0 = row0 // tt, row0 // halo
    vec = lambda a: a.reshape(1, width).astype(F32)
    const = lambda i: (0, 0)
    return pl.pallas_call(
        functools.partial(_conformer_kernel, taps=taps, tiles_per_seq=tiles_per_seq),
        grid=(rows // tt,),
        in_specs=[
            pl.BlockSpec((tt, width), lambda i: (b0 + i, 0)),
            pl.BlockSpec((halo, width), lambda i: (jnp.maximum(hb0 + i * (tt // halo) - 1, 0), 0)),
            pl.BlockSpec((taps, width), const),
            pl.BlockSpec((1, width), const), pl.BlockSpec((1, width), const), pl.BlockSpec((1, width), const),
        ],
        out_specs=pl.BlockSpec((tt, width), lambda i: (i, 0)),
        out_shape=jax.ShapeDtypeStruct((rows, width), BF16),
        scratch_shapes=[pltpu.VMEM((tt + halo, width), F32), pltpu.VMEM((tt, width), F32)],
        compiler_params=_cparams("parallel"),
        name="conformer_conv",
    )(u, u, dw_w, vec(dw_b), vec(ln_g), vec(ln_b))


def _router_kernel(x_ref, g_ref, w_ref, o_ref, *, n_experts):
    x = x_ref[...]
    ms = jnp.mean(x * x, axis=-1, keepdims=True)
    xn = x * lax.rsqrt(ms + NORM_EPS) * g_ref[...]
    logits = _dot(xn, w_ref[...], HIGHEST)
    lane = lax.broadcasted_iota(jnp.int32, logits.shape, 1)
    logits = jnp.where(lane < n_experts, logits, -jnp.inf)
    e = jnp.exp(logits - jnp.max(logits, axis=-1, keepdims=True))
    probs = e / jnp.sum(e, axis=-1, keepdims=True)
    p1 = jnp.max(probs, axis=-1, keepdims=True)
    i1 = jnp.min(jnp.where(probs == p1, lane, LANES), axis=-1, keepdims=True)
    rest = jnp.where(lane == i1, -1.0, probs)
    p2 = jnp.max(rest, axis=-1, keepdims=True)
    i2 = jnp.min(jnp.where(rest == p2, lane, LANES), axis=-1, keepdims=True)
    denom = p1 + p2
    o_ref[...] = jnp.where(lane == i1, p1 / denom, 0.0) + jnp.where(lane == i2, p2 / denom, 0.0)


def moe_gates(x, norm_g, router_w, *, tm=256):
    M, D = x.shape
    n_experts = router_w.shape[1]
    w_pad = jnp.zeros((D, LANES), F32).at[:, :n_experts].set(router_w)
    return pl.pallas_call(
        functools.partial(_router_kernel, n_experts=n_experts),
        grid=(M // tm,),
        in_specs=[pl.BlockSpec((tm, D), lambda i: (i, 0)), pl.BlockSpec((1, D), lambda i: (0, 0)),
                  pl.BlockSpec((D, LANES), lambda i: (0, 0))],
        out_specs=pl.BlockSpec((tm, LANES), lambda i: (i, 0)),
        out_shape=jax.ShapeDtypeStruct((M, LANES), F32),
        compiler_params=_cparams("parallel"),
        name="moe_router",
    )(x, norm_g.reshape(1, D), w_pad)


def _swiglu_epilogue(dots, extras):
    return _silu(dots[0]) * dots[1]


def _rope_tables(positions, rope_d):
    half = rope_d // 2
    inv_freq = jnp.power(ROPE_THETA, -jnp.arange(half, dtype=F32) / half)
    ang = positions[:, None] * inv_freq[None, :]
    cos, sin = jnp.cos(ang), jnp.sin(ang)
    return jnp.concatenate([cos, cos], axis=-1), jnp.concatenate([-sin, sin], axis=-1)


def kernel(x_prompt, x_sample, cache_kv_latent, cache_k_rope, state_gdn, state_gdn_conv, state_conf_conv, page_table, attn_norm, w_in, q_norm, kv_norm, w_q_b, w_uk, w_uv, gdn_conv_w, gdn_A_log, gdn_dt_bias, gdn_o_norm, conf_dw_w, conf_dw_b, conf_ln_g, conf_ln_b, w_o_mla, w_o_gdn, w_o_conf, w_out, ffn_norm, ffn_w1, ffn_w3, ffn_w2, moe_router, moe_w1, moe_w3, moe_w2, final_norm):
    B, T, D = x_prompt.shape
    Bd, Td, _ = x_sample.shape
    depth = w_in.shape[0]
    q_lora = q_norm.shape[1]
    kv_lora = kv_norm.shape[1]
    H, nope = w_uk.shape[2], w_uk.shape[3]
    rope_d = w_q_b.shape[3] - nope
    v_head = w_uv.shape[3]
    gdn_qkv = gdn_conv_w.shape[2]
    gdn_val = gdn_qkv // 3
    GH = gdn_A_log.shape[1]
    conf_c = conf_dw_w.shape[2]
    conf_k = conf_dw_w.shape[1]
    gdn_k = gdn_conv_w.shape[1]
    n_pages = page_table.shape[1]
    past_len = n_pages * cache_kv_latent.shape[2]
    Mp, Ms = B * T, Bd * Td
    M = Mp + Ms
    TM = 1024
    assert M % TM == 0 and Td == 8 and q_lora % kv_lora == 0 and (q_lora + kv_lora) % (3 * LANES) == 0

    sizes = (q_lora, kv_lora, rope_d, gdn_qkv, gdn_val, GH, GH, 2 * conf_c, 3 * D)
    offs = [0]
    for s in sizes:
        offs.append(offs[-1] + s)
    o_cq, o_ckv, o_kpe, o_qkv, o_z, o_beta, o_alpha, o_glu, o_gate, _ = offs

    X = jnp.concatenate([x_prompt.reshape(Mp, D), x_sample.reshape(Ms, D)], axis=0)
    pos = jnp.concatenate([jnp.tile(jnp.arange(T, dtype=F32), B), jnp.tile(past_len + jnp.arange(Td, dtype=F32), Bd)])
    cosf, sinf = _rope_tables(pos, rope_d)
    mla_scale = (nope + rope_d) ** -0.5
    zero_S = jnp.zeros((B, GH, LANES, LANES), F32)

    def lane_pad(w, width):
        return jnp.pad(w, ((0, 0), (0, width - w.shape[1])))

    outs = {k: [] for k in ("kv_p", "kr_p", "S_p", "gc_p", "cc_p", "kv_s", "kr_s", "S_s", "gc_s", "cc_s")}
    small_w = q_lora + kv_lora + 3 * LANES
    beta_col = (q_lora + kv_lora) // LANES + 1
    alpha_col = beta_col + 1

    for l in range(depth):
        wl = w_in[l]
        w_small = jnp.concatenate([
            wl[:, o_cq:o_kpe], lane_pad(wl[:, o_kpe:o_qkv], LANES), lane_pad(wl[:, o_beta:o_alpha], LANES),
            lane_pad(wl[:, o_alpha:o_glu], LANES)], axis=1).astype(BF16)
        w_qkv = wl[:, o_qkv:o_z].astype(BF16)
        w_z = wl[:, o_z:o_beta].astype(BF16)
        w_glu = wl[:, o_glu:o_gate].astype(BF16)
        w_gate = wl[:, o_gate:].astype(BF16)

        xn = rmsnorm(X, attn_norm[l], BF16)
        y_small = matmul(xn, w_small, F32, tm=TM, tn=small_w // 3, name="proj_small")
        qkv = matmul(xn, w_qkv, F32, tm=TM, tn=1024, name="proj_qkv")
        z = matmul(xn, w_z, F32, tm=TM, tn=1024, name="proj_z")
        tn_glu = 512
        u = matmul(xn, w_glu, F32, tm=TM, tn=tn_glu, n_out=conf_c, w_col_blocks=(0, conf_c // tn_glu),
                   epilogue=lambda dots, ex: dots[0] * _sigmoid(dots[1]), name="proj_glu")
        gates = matmul(xn, w_gate, BF16, tm=TM, tn=1024, epilogue=lambda dots, ex: _sigmoid(dots[0]), name="proj_gates")

        wq = w_q_b[l].astype(BF16)
        wqn = jnp.transpose(wq[:, :, :nope], (1, 0, 2))
        wqp = jnp.transpose(wq[:, :, nope:], (1, 0, 2))
        wukT = jnp.transpose(w_uk[l].astype(BF16), (1, 2, 0))
        wuvT = jnp.transpose(w_uv[l].astype(BF16), (1, 0, 2))
        q_lat, q_pe = mla_q(y_small, q_norm[l], wqn, wqp, wukT, cosf, sinf, q_lora=q_lora, scale=mla_scale)
        ckv, kpe, ckv_bf, kpe_bf = mla_kv(y_small, kv_norm[l], cosf, sinf, q_lora=q_lora, kv_lora=kv_lora)
        o_lat = mla_prompt_attention(q_lat, q_pe, ckv_bf, kpe_bf, B=B, T=T, m_total=M)

        def to_sample_rows(a):
            return jnp.transpose(a[:, Mp:, :].reshape(H, Bd, Td, -1), (1, 0, 2, 3)).reshape(Bd, H * Td, -1)

        o_lat_s = mla_sample_attention(to_sample_rows(q_lat), to_sample_rows(q_pe), cache_kv_latent, cache_k_rope,
                                       page_table, ckv, kpe, layer=l, new_row0=Mp)
        o_lat_s = jnp.transpose(o_lat_s.reshape(Bd, H, Td, kv_lora), (1, 0, 2, 3)).reshape(H, Ms, kv_lora)
        o_lat = lax.dynamic_update_slice(o_lat, o_lat_s, (0, Mp, 0))
        o_mla = fused_matmul(
            [(o_lat, pl.BlockSpec((None, TM, kv_lora), lambda i, h, k: (h, i, 0)))],
            [(wuvT, pl.BlockSpec((None, kv_lora, v_head), lambda i, h, k: (h, 0, 0)))],
            [0], [], lambda dots, ex: dots[0], jax.ShapeDtypeStruct((M, H * v_head), BF16),
            pl.BlockSpec((TM, v_head), lambda i, h, k: (i, h)), (M // TM, H, 1), None, "mla_uv")

        gconv = gdn_conv_w[l]
        grp = 2 * Td
        qkv_s = qkv[Mp:].reshape(Bd, Td, gdn_qkv)
        qkv_s_pad = jnp.concatenate([jnp.zeros((Bd, grp - Td - (gdn_k - 1), gdn_qkv), F32), state_gdn_conv[l], qkv_s],
                                    axis=1).reshape(Bd * grp, gdn_qkv)
        qkvn_p = gdn_prep(qkv, gconv, row0=0, rows=Mp, tiles_per_seq=T // 256, tt=256, key_width=gdn_val)
        qkvn_s = gdn_prep(qkv_s_pad, gconv, row0=0, rows=Bd * grp, tiles_per_seq=Bd * grp, tt=256, key_width=gdn_val)
        gdn_args = (gdn_A_log[l], gdn_dt_bias[l], gdn_o_norm[l])
        o_gdn_p, S_p = gdn_scan(qkvn_p, y_small, z, *gdn_args, zero_S, n_seq=B, T=T, C=math.gcd(T, 64),
                                qkv_blk0=0, qkv_blk_stride=1, tok_row0=0, beta_col=beta_col, alpha_col=alpha_col)
        o_gdn_s, S_s = gdn_scan(qkvn_s, y_small, z, *gdn_args, state_gdn[l], n_seq=Bd, T=Td, C=math.gcd(Td, 64),
                                qkv_blk0=1, qkv_blk_stride=2, tok_row0=Mp, beta_col=beta_col, alpha_col=alpha_col)
        o_gdn = jnp.concatenate([o_gdn_p, o_gdn_s], axis=0).astype(BF16)

        cgrp = 40
        u_s = u[Mp:].reshape(Bd, Td, conf_c)
        u_s_pad = jnp.concatenate([jnp.zeros((Bd, cgrp - Td - (conf_k - 1), conf_c), F32), state_conf_conv[l], u_s],
                                  axis=1).reshape(Bd * cgrp, conf_c)
        conf_args = (conf_dw_w[l], conf_dw_b[l], conf_ln_g[l], conf_ln_b[l])
        h_conf_p = conformer_conv(u, *conf_args, row0=0, rows=Mp, tiles_per_seq=T // 256, tt=256)
        h_conf_s = conformer_conv(u_s_pad, *conf_args, row0=0, rows=Bd * cgrp, tiles_per_seq=Bd * cgrp, tt=320)
        h_conf = jnp.concatenate([h_conf_p, h_conf_s.reshape(Bd, cgrp, conf_c)[:, cgrp - Td:].reshape(Ms, conf_c)], axis=0)

        tn_m = 512
        kspec = lambda width: pl.BlockSpec((TM, width), lambda i, j, k: (i, 0))
        wspec = lambda width: pl.BlockSpec((width, tn_m), lambda i, j, k: (0, j))
        merged = fused_matmul(
            [(o_mla, kspec(H * v_head)), (o_gdn, kspec(gdn_val)), (h_conf, kspec(conf_c))],
            [(w_o_mla[l].astype(BF16), wspec(H * v_head)), (w_o_gdn[l].astype(BF16), wspec(gdn_val)),
             (w_o_conf[l].astype(BF16), wspec(conf_c))],
            [0, 1, 2],
            [(gates, _tile_spec(TM, tn_m, br * (D // tn_m))) for br in range(3)],
            lambda dots, ex: sum(ex[br].astype(F32) * dots[br] for br in range(3)),
            jax.ShapeDtypeStruct((M, D), BF16), _tile_spec(TM, tn_m), (M // TM, D // tn_m, 1), None, "merge")
        res_ep = lambda dots, ex: ex[0] + dots[0]
        X = matmul(merged, w_out[l].astype(BF16), F32, tm=TM, tn=512, epilogue=res_ep,
                   extras=[(X, _tile_spec(TM, 512))], name="w_out")

        i = l // 2
        if l % 2 == 0:
            xn2 = rmsnorm(X, ffn_norm[l], BF16)
            d_ff = ffn_w1.shape[2]
            h = fused_matmul(
                [(xn2, pl.BlockSpec((TM, D), lambda i_, j, k: (i_, 0)))],
                [(ffn_w1[i].astype(BF16), pl.BlockSpec((D, 512), lambda i_, j, k: (0, j))),
                 (ffn_w3[i].astype(BF16), pl.BlockSpec((D, 512), lambda i_, j, k: (0, j)))],
                [0, 0], [], _swiglu_epilogue, jax.ShapeDtypeStruct((M, d_ff), BF16), _tile_spec(TM, 512),
                (M // TM, d_ff // 512, 1), None, "ffn_up")
            X = matmul(h, ffn_w2[i].astype(BF16), F32, tm=TM, tn=512, tk=4096, epilogue=res_ep,
                       extras=[(X, _tile_spec(TM, 512))], name="ffn_down")
        else:
            xn2 = rmsnorm(X, ffn_norm[l], BF16)
            gate = moe_gates(X, ffn_norm[l], moe_router[i])
            E, _, d_e = moe_w1.shape[1:]
            per_e = d_e // 512

            def gated_swiglu(dots, ex):
                e = pl.program_id(1) // per_e
                lane = lax.broadcasted_iota(jnp.int32, ex[0].shape, 1)
                g = jnp.sum(jnp.where(lane == e, ex[0], 0.0), axis=1, keepdims=True)
                return g * _silu(dots[0]) * dots[1]

            wmap = lambda i_, j, k: (j // per_e, 0, j % per_e)
            h = fused_matmul(
                [(xn2, pl.BlockSpec((TM, D), lambda i_, j, k: (i_, 0)))],
                [(moe_w1[i].astype(BF16), pl.BlockSpec((None, D, 512), wmap)),
                 (moe_w3[i].astype(BF16), pl.BlockSpec((None, D, 512), wmap))],
                [0, 0], [(gate, pl.BlockSpec((TM, LANES), lambda i_, j, k: (i_, 0)))], gated_swiglu,
                jax.ShapeDtypeStruct((M, E * d_e), BF16), _tile_spec(TM, 512), (M // TM, E * d_e // 512, 1), None, "moe_up")
            X = matmul(h, moe_w2[i].astype(BF16).reshape(E * d_e, D), F32, tm=TM, tn=512, tk=4096, epilogue=res_ep,
                       extras=[(X, _tile_spec(TM, 512))], name="moe_down")

        outs["kv_p"].append(ckv[:Mp].reshape(B, T, kv_lora))
        outs["kr_p"].append(kpe[:Mp].reshape(B, T, rope_d))
        outs["S_p"].append(S_p)
        outs["gc_p"].append(qkv[:Mp].reshape(B, T, gdn_qkv)[:, T - (gdn_k - 1):])
        outs["cc_p"].append(u[:Mp].reshape(B, T, conf_c)[:, T - (conf_k - 1):])
        outs["kv_s"].append(ckv[Mp:].reshape(Bd, Td, kv_lora))
        outs["kr_s"].append(kpe[Mp:].reshape(Bd, Td, rope_d))
        outs["S_s"].append(S_s)
        outs["gc_s"].append(jnp.concatenate([state_gdn_conv[l], qkv_s], axis=1)[:, Td:])
        outs["cc_s"].append(jnp.concatenate([state_conf_conv[l], u_s], axis=1)[:, Td:])

    y_p = rmsnorm(X, final_norm, F32, row0=0, rows=Mp).reshape(B, T, D)
    y_s = rmsnorm(X, final_norm, F32, row0=Mp, rows=Ms).reshape(Bd, Td, D)
    st = {k: jnp.stack(v) for k, v in outs.items()}
    return (y_p, y_s, st["kv_p"], st["kr_p"], st["S_p"], st["gc_p"], st["cc_p"],
            st["kv_s"], st["kr_s"], st["S_s"], st["gc_s"], st["cc_s"])
```

```python
import functools
import math

import jax
import jax.numpy as jnp
from jax import lax
from jax.experimental import pallas as pl
from jax.experimental.pallas import tpu as pltpu

F32 = jnp.float32
BF16 = jnp.bfloat16
HIGHEST = lax.Precision.HIGHEST
NORM_EPS = 1e-6
ROPE_THETA = 10000.0
VMEM_LIMIT_BYTES = 56 * 1024 * 1024
LANES = 128
TRI_BLOCK = 16


def _cparams(*sem):
    return pltpu.CompilerParams(dimension_semantics=sem, vmem_limit_bytes=VMEM_LIMIT_BYTES)


def _dot(a, b, precision=None):
    return jnp.dot(a, b, preferred_element_type=F32, precision=precision)


def _dot_nt(a, b, precision=None):
    return lax.dot_general(a, b, (((1,), (1,)), ((), ())), preferred_element_type=F32, precision=precision)


def _dot_tn(a, b, precision=None):
    return lax.dot_general(a, b, (((0,), (0,)), ((), ())), preferred_element_type=F32, precision=precision)


def _split_bf16(a):
    hi = a.astype(BF16)
    return hi, (a - hi.astype(F32)).astype(BF16)


def _dot3(a, b):
    return _dot(a[0], b[0]) + (_dot(a[0], b[1]) + _dot(a[1], b[0]))


def _sigmoid(x):
    return 1.0 / (1.0 + jnp.exp(-x))


def _silu(x):
    return x * _sigmoid(x)


def _rmsnorm_kernel(x_ref, g_ref, o_ref):
    x = x_ref[...]
    ms = jnp.mean(x * x, axis=-1, keepdims=True)
    o_ref[...] = (x * lax.rsqrt(ms + NORM_EPS) * g_ref[...]).astype(o_ref.dtype)


def rmsnorm(x, g, out_dtype, *, row0=0, rows=None, tm=256):
    M, D = x.shape
    rows = M - row0 if rows is None else rows
    tm = min(tm, rows)
    b0 = row0 // tm
    return pl.pallas_call(
        _rmsnorm_kernel,
        grid=(rows // tm,),
        in_specs=[pl.BlockSpec((tm, D), lambda i: (b0 + i, 0)), pl.BlockSpec((1, D), lambda i: (0, 0))],
        out_specs=pl.BlockSpec((tm, D), lambda i: (i, 0)),
        out_shape=jax.ShapeDtypeStruct((rows, D), out_dtype),
        compiler_params=_cparams("parallel"),
        name="rmsnorm",
    )(x, g.reshape(1, D))


def fused_matmul(a_ops, w_ops, amap, e_ops, epilogue, out_shape, out_spec, grid, acc_shape, name):
    na, nd, ne = len(a_ops), len(w_ops), len(e_ops)
    nk = grid[2]

    def kernel(*refs):
        a_refs = refs[:na]
        w_refs = refs[na:na + nd]
        e_refs = refs[na + nd:na + nd + ne]
        o_ref = refs[na + nd + ne]
        acc_refs = refs[na + nd + ne + 1:]
        dots = [_dot(a_refs[amap[d]][...], w_refs[d][...]) for d in range(nd)]
        if nk == 1:
            o_ref[...] = epilogue(dots, [e[...] for e in e_refs]).astype(o_ref.dtype)
        else:
            k = pl.program_id(2)

            @pl.when(k == 0)
            def _():
                for d in range(nd):
                    acc_refs[d][...] = dots[d]

            @pl.when(k > 0)
            def _():
                for d in range(nd):
                    acc_refs[d][...] += dots[d]

            @pl.when(k == nk - 1)
            def _():
                o_ref[...] = epilogue([acc[...] for acc in acc_refs], [e[...] for e in e_refs]).astype(o_ref.dtype)

    scratch = [pltpu.VMEM(acc_shape, F32) for _ in range(nd)] if nk > 1 else []
    ops = a_ops + w_ops + e_ops
    return pl.pallas_call(
        kernel,
        grid=grid,
        in_specs=[s for _, s in ops],
        out_specs=out_spec,
        out_shape=out_shape,
        scratch_shapes=scratch,
        compiler_params=_cparams("parallel", "parallel", "arbitrary"),
        name=name,
    )(*[a for a, _ in ops])


def matmul(a, w, out_dtype, *, tm, tn, tk=None, epilogue=None, extras=(), n_out=None, w_col_blocks=(0,), name="matmul"):
    M, K = a.shape
    n_out = w.shape[1] if n_out is None else n_out
    tk = K if tk is None else tk
    grid = (M // tm, n_out // tn, K // tk)
    a_ops = [(a, pl.BlockSpec((tm, tk), lambda i, j, k: (i, k)))]
    w_ops = [(w, pl.BlockSpec((tk, tn), functools.partial(lambda i, j, k, off: (k, j + off), off=off)))
             for off in w_col_blocks]
    if epilogue is None:
        epilogue = lambda dots, ex: dots[0]
    return fused_matmul(a_ops, w_ops, [0] * len(w_ops), list(extras), epilogue,
                        jax.ShapeDtypeStruct((M, n_out), out_dtype),
                        pl.BlockSpec((tm, tn), lambda i, j, k: (i, j)), grid, (tm, tn), name)


def _tile_spec(tm, tn, col_off=0):
    return pl.BlockSpec((tm, tn), lambda i, j, k: (i, j + col_off))


def _rope_rows(x, cosf, sinf):
    half = x.shape[-1] // 2
    swapped = jnp.concatenate([x[:, half:], x[:, :half]], axis=-1)
    return x * cosf + swapped * sinf


def _mla_q_kernel(cq_ref, qn_ref, wqn_ref, wqp_ref, wuk_ref, cos_ref, sin_ref, ql_ref, qp_ref, cqn_sc, *, scale):
    @pl.when(pl.program_id(1) == 0)
    def _():
        x = cq_ref[...]
        ms = jnp.mean(x * x, axis=-1, keepdims=True)
        cqn_sc[...] = (x * lax.rsqrt(ms + NORM_EPS) * qn_ref[...]).astype(BF16)

    cqn = cqn_sc[...]
    q_nope = _dot(cqn, wqn_ref[...])
    q_pe = _dot(cqn, wqp_ref[...])
    q_lat = _dot(q_nope.astype(BF16), wuk_ref[...])
    ql_ref[...] = (q_lat * scale).astype(BF16)
    qp_ref[...] = (_rope_rows(q_pe, cos_ref[...], sin_ref[...]) * scale).astype(BF16)


def mla_q(y_small, q_norm, wqn, wqp, wukT, cosf, sinf, *, q_lora, scale, tm=512):
    M = y_small.shape[0]
    H, _, nope = wqn.shape
    rope_d = wqp.shape[2]
    kv_lora = wukT.shape[2]
    return pl.pallas_call(
        functools.partial(_mla_q_kernel, scale=scale),
        grid=(M // tm, H),
        in_specs=[
            pl.BlockSpec((tm, q_lora), lambda i, h: (i, 0)),
            pl.BlockSpec((1, q_lora), lambda i, h: (0, 0)),
            pl.BlockSpec((None, q_lora, nope), lambda i, h: (h, 0, 0)),
            pl.BlockSpec((None, q_lora, rope_d), lambda i, h: (h, 0, 0)),
            pl.BlockSpec((None, nope, kv_lora), lambda i, h: (h, 0, 0)),
            pl.BlockSpec((tm, rope_d), lambda i, h: (i, 0)),
            pl.BlockSpec((tm, rope_d), lambda i, h: (i, 0)),
        ],
        out_specs=[pl.BlockSpec((None, tm, kv_lora), lambda i, h: (h, i, 0)),
                   pl.BlockSpec((None, tm, rope_d), lambda i, h: (h, i, 0))],
        out_shape=[jax.ShapeDtypeStruct((H, M, kv_lora), BF16), jax.ShapeDtypeStruct((H, M, rope_d), BF16)],
        scratch_shapes=[pltpu.VMEM((tm, q_lora), BF16)],
        compiler_params=_cparams("parallel", "arbitrary"),
        name="mla_q",
    )(y_small, q_norm.reshape(1, -1), wqn, wqp, wukT, cosf, sinf)


def _mla_kv_kernel(ckv_ref, kpe_ref, g_ref, cos_ref, sin_ref, ckv_o, kpe_o, ckvb_o, kpeb_o):
    x = ckv_ref[...]
    ms = jnp.mean(x * x, axis=-1, keepdims=True)
    c = x * lax.rsqrt(ms + NORM_EPS) * g_ref[...]
    rope_d = kpe_o.shape[-1]
    kr = _rope_rows(kpe_ref[...][:, :rope_d], cos_ref[...], sin_ref[...])
    ckv_o[...] = c
    kpe_o[...] = kr
    ckvb_o[...] = c.astype(BF16)
    kpeb_o[...] = kr.astype(BF16)


def mla_kv(y_small, kv_norm, cosf, sinf, *, q_lora, kv_lora, tm=512):
    M = y_small.shape[0]
    rope_d = cosf.shape[1]
    row = lambda i: (i, 0)
    return pl.pallas_call(
        _mla_kv_kernel,
        grid=(M // tm,),
        in_specs=[
            pl.BlockSpec((tm, kv_lora), lambda i: (i, q_lora // kv_lora)),
            pl.BlockSpec((tm, LANES), lambda i: (i, (q_lora + kv_lora) // LANES)),
            pl.BlockSpec((1, kv_lora), lambda i: (0, 0)),
            pl.BlockSpec((tm, rope_d), row),
            pl.BlockSpec((tm, rope_d), row),
        ],
        out_specs=[pl.BlockSpec((tm, kv_lora), row), pl.BlockSpec((tm, rope_d), row),
                   pl.BlockSpec((tm, kv_lora), row), pl.BlockSpec((tm, rope_d), row)],
        out_shape=[jax.ShapeDtypeStruct((M, kv_lora), F32), jax.ShapeDtypeStruct((M, rope_d), F32),
                   jax.ShapeDtypeStruct((M, kv_lora), BF16), jax.ShapeDtypeStruct((M, rope_d), BF16)],
        compiler_params=_cparams("parallel"),
        name="mla_kv",
    )(y_small, y_small, kv_norm.reshape(1, -1), cosf, sinf)


def _softmax_update(s, v_bf, m_sc, l_sc, acc_sc):
    m_prev = m_sc[...]
    m_new = jnp.maximum(m_prev, jnp.max(s, axis=-1, keepdims=True))
    p = jnp.exp(s - m_new)
    corr = jnp.exp(m_prev - m_new)
    l_sc[...] = l_sc[...] * corr + jnp.sum(p, axis=-1, keepdims=True)
    acc_sc[...] = acc_sc[...] * corr + _dot(p.astype(BF16), v_bf)
    m_sc[...] = m_new


def _prompt_attn_kernel(ql_ref, qp_ref, k_ref, kr_ref, o_ref, m_sc, l_sc, acc_sc, *, tq, tk):
    i = pl.program_id(1)
    j = pl.program_id(2)
    H = ql_ref.shape[0]
    R = H * tq
    j_last = (i * tq + tq - 1) // tk

    @pl.when(j == 0)
    def _():
        m_sc[...] = jnp.full(m_sc.shape, -jnp.inf, F32)
        l_sc[...] = jnp.zeros(l_sc.shape, F32)
        acc_sc[...] = jnp.zeros(acc_sc.shape, F32)

    @pl.when(j <= j_last)
    def _():
        ql = ql_ref[...].reshape(R, ql_ref.shape[-1])
        qp = qp_ref[...].reshape(R, qp_ref.shape[-1])
        k = k_ref[...]
        s = _dot_nt(ql, k) + _dot_nt(qp, kr_ref[...])
        q_pos = i * tq + lax.broadcasted_iota(jnp.int32, (R, tk), 0) % tq
        k_pos = j * tk + lax.broadcasted_iota(jnp.int32, (R, tk), 1)
        s = jnp.where(k_pos <= q_pos, s, -jnp.inf)
        _softmax_update(s, k, m_sc, l_sc, acc_sc)

    @pl.when(j == j_last)
    def _():
        o = acc_sc[...] / l_sc[...]
        o_ref[...] = o.astype(BF16).reshape(o_ref.shape)


def mla_prompt_attention(q_lat, q_pe, ckv_bf, kpe_bf, *, B, T, m_total, tq=128, tk=512):
    H, _, C = q_lat.shape
    rope_d = q_pe.shape[2]
    nq, nk = T // tq, T // tk

    def kmap(b, i, j):
        return (b * nk + jnp.minimum(j, (i * tq + tq - 1) // tk), 0)

    qmap = lambda b, i, j: (0, b * nq + i, 0)
    return pl.pallas_call(
        functools.partial(_prompt_attn_kernel, tq=tq, tk=tk),
        grid=(B, nq, nk),
        in_specs=[pl.BlockSpec((H, tq, C), qmap), pl.BlockSpec((H, tq, rope_d), qmap),
                  pl.BlockSpec((tk, C), kmap), pl.BlockSpec((tk, rope_d), kmap)],
        out_specs=pl.BlockSpec((H, tq, C), qmap),
        out_shape=jax.ShapeDtypeStruct((H, m_total, C), BF16),
        scratch_shapes=[pltpu.VMEM((H * tq, 1), F32), pltpu.VMEM((H * tq, 1), F32), pltpu.VMEM((H * tq, C), F32)],
        compiler_params=_cparams("parallel", "parallel", "arbitrary"),
        name="mla_prompt_attn",
    )(q_lat, q_pe, ckv_bf, kpe_bf)


def _sample_attn_kernel(pt_ref, ql_ref, qp_ref, *refs, pps, page, t_new):
    kv_refs = refs[:pps]
    kr_refs = refs[pps:2 * pps]
    nkv_ref, nkr_ref, o_ref, kbuf, rbuf, m_sc, l_sc, acc_sc = refs[2 * pps:]
    s_idx = pl.program_id(1)

    @pl.when(s_idx == 0)
    def _():
        m_sc[...] = jnp.full(m_sc.shape, -jnp.inf, F32)
        l_sc[...] = jnp.zeros(l_sc.shape, F32)
        acc_sc[...] = jnp.zeros(acc_sc.shape, F32)

    for j in range(pps):
        kbuf[j * page:(j + 1) * page, :] = kv_refs[j][...].astype(BF16)
        rbuf[:, j * page:(j + 1) * page] = kr_refs[j][...].astype(BF16)
    ql = ql_ref[...]
    qp = qp_ref[...]
    kb = kbuf[...]
    s = _dot_nt(ql, kb) + _dot(qp, rbuf[...])
    _softmax_update(s, kb, m_sc, l_sc, acc_sc)

    @pl.when(s_idx == pl.num_programs(1) - 1)
    def _():
        nk = nkv_ref[...].astype(BF16)
        R = ql.shape[0]
        s_new = _dot_nt(ql, nk) + _dot_nt(qp, nkr_ref[...].astype(BF16))
        q_t = lax.broadcasted_iota(jnp.int32, (R, t_new), 0) % t_new
        k_t = lax.broadcasted_iota(jnp.int32, (R, t_new), 1)
        s_new = jnp.where(k_t <= q_t, s_new, -jnp.inf)
        _softmax_update(s_new, nk, m_sc, l_sc, acc_sc)
        o_ref[...] = (acc_sc[...] / l_sc[...]).astype(BF16)


def mla_sample_attention(q_lat_s, q_pe_s, cache_kv, cache_krT, page_table, ckv_new, kpe_new, *, layer, new_row0, pps=16):
    Bd, R, C = q_lat_s.shape
    rope_d = q_pe_s.shape[2]
    n_pages = page_table.shape[1]
    page = cache_kv.shape[2]
    t_new = 8
    assert n_pages % pps == 0 and new_row0 % t_new == 0
    pt_flat = page_table.reshape(-1)

    def page_map(b, s, pt, *, j):
        return (layer, pt[b * n_pages + s * pps + j], 0, 0)

    qmap = lambda b, s, pt: (b, 0, 0)
    newmap = lambda b, s, pt: (new_row0 // t_new + b, 0)
    in_specs = [pl.BlockSpec((None, R, C), qmap), pl.BlockSpec((None, R, rope_d), qmap)]
    in_specs += [pl.BlockSpec((None, None, page, C), functools.partial(page_map, j=j)) for j in range(pps)]
    in_specs += [pl.BlockSpec((None, None, rope_d, page), functools.partial(page_map, j=j)) for j in range(pps)]
    in_specs += [pl.BlockSpec((t_new, C), newmap), pl.BlockSpec((t_new, rope_d), newmap)]
    grid_spec = pltpu.PrefetchScalarGridSpec(
        num_scalar_prefetch=1,
        grid=(Bd, n_pages // pps),
        in_specs=in_specs,
        out_specs=pl.BlockSpec((None, R, C), qmap),
        scratch_shapes=[pltpu.VMEM((pps * page, C), BF16), pltpu.VMEM((rope_d, pps * page), BF16),
                        pltpu.VMEM((R, 1), F32), pltpu.VMEM((R, 1), F32), pltpu.VMEM((R, C), F32)],
    )
    return pl.pallas_call(
        functools.partial(_sample_attn_kernel, pps=pps, page=page, t_new=t_new),
        grid_spec=grid_spec,
        out_shape=jax.ShapeDtypeStruct((Bd, R, C), BF16),
        compiler_params=_cparams("parallel", "arbitrary"),
        name="mla_sample_attn",
    )(pt_flat, q_lat_s, q_pe_s, *([cache_kv] * pps), *([cache_krT] * pps), ckv_new, kpe_new)


def _gdn_prep_kernel(x_ref, halo_ref, w_ref, o_ref, *, taps, tiles_per_seq, q_scale):
    i = pl.program_id(0)
    kind = pl.program_id(1)
    halo_rows = halo_ref.shape[0]
    tt = x_ref.shape[0]
    keep = (i % tiles_per_seq != 0).astype(F32)
    xx = jnp.concatenate([halo_ref[...] * keep, x_ref[...]], axis=0)
    w = w_ref[...]
    y = jnp.zeros(x_ref.shape, F32)
    for k in range(taps):
        off = halo_rows - (taps - 1) + k
        y = y + w[k:k + 1, :] * xx[off:off + tt, :]
    y = _silu(y)
    scale = jnp.where(kind == 0, q_scale, 1.0).astype(F32)
    for h in range(x_ref.shape[1] // LANES):
        yh = y[:, h * LANES:(h + 1) * LANES]
        nrm = yh * lax.rsqrt(jnp.sum(yh * yh, axis=-1, keepdims=True) + NORM_EPS) * scale
        o_ref[:, h * LANES:(h + 1) * LANES] = jnp.where(kind < 2, nrm, yh)


def gdn_prep(qkv, conv_w, *, row0, rows, tiles_per_seq, tt, key_width):
    taps, width = conv_w.shape
    halo = 8
    assert width == 3 * key_width and row0 % tt == 0 and rows % tt == 0
    b0, hb0 = row0 // tt, row0 // halo
    return pl.pallas_call(
        functools.partial(_gdn_prep_kernel, taps=taps, tiles_per_seq=tiles_per_seq, q_scale=LANES ** -0.5),
        grid=(rows // tt, 3),
        in_specs=[
            pl.BlockSpec((tt, key_width), lambda i, c: (b0 + i, c)),
            pl.BlockSpec((halo, key_width), lambda i, c: (jnp.maximum(hb0 + i * (tt // halo) - 1, 0), c)),
            pl.BlockSpec((taps, key_width), lambda i, c: (0, c)),
        ],
        out_specs=pl.BlockSpec((tt, key_width), lambda i, c: (i, c)),
        out_shape=jax.ShapeDtypeStruct((rows, width), F32),
        compiler_params=_cparams("parallel", "parallel"),
        name="gdn_prep",
    )(qkv, qkv, conv_w)


def _unit_lower_inverse(L, C, eye, blockdiag):
    blk = min(C, TRI_BLOCK)
    D = L if C <= TRI_BLOCK else jnp.where(blockdiag, L, 0.0)
    P = -D
    T = eye + P
    for _ in range(int(math.log2(blk)) - 1):
        Ps = _split_bf16(P)
        P = _dot3(Ps, Ps)
        T = T + _dot3(_split_bf16(T), _split_bf16(P))
    if C > TRI_BLOCK:
        assert C // TRI_BLOCK <= 4
        Ts = _split_bf16(T)
        Mx = _dot3(Ts, _split_bf16(L - D))
        Ms = _split_bf16(Mx)
        M2 = _dot3(Ms, Ms)
        ImM = eye - Mx
        R = ImM + _dot3(_split_bf16(ImM), _split_bf16(M2))
        T = _dot3(_split_bf16(R), Ts)
    return T


def _gdn_scan_kernel(q_ref, k_ref, v_ref, br_ref, ar_ref, z_ref, alog_ref, dtb_ref, on_ref, s0_ref,
                     o_ref, s_out_ref, s_sc, *, hb, C):
    c = pl.program_id(2)
    hg = pl.program_id(1)

    @pl.when(c == 0)
    def _():
        s_sc[...] = s0_ref[...]

    row = lax.broadcasted_iota(jnp.int32, (C, C), 0)
    col = lax.broadcasted_iota(jnp.int32, (C, C), 1)
    incl = row >= col
    strict = row > col
    eye = (row == col).astype(F32)
    blockdiag = (row // TRI_BLOCK) == (col // TRI_BLOCK)

    beta_all = _sigmoid(br_ref[...])
    x = ar_ref[...] + dtb_ref[...]
    softplus = jnp.maximum(x, 0.0) + jnp.log(1.0 + jnp.exp(-jnp.abs(x)))
    g_all = -jnp.exp(alog_ref[...]) * softplus
    gc_all = _dot(incl.astype(F32), g_all, HIGHEST)
    gr_all = _dot_tn(g_all, (row <= col).astype(F32), HIGHEST)
    lane = lax.broadcasted_iota(jnp.int32, (C, LANES), 1)
    sub = lax.broadcasted_iota(jnp.int32, (LANES, C), 0)

    for hh in range(hb):
        head = hg * hb + hh
        sl = slice(hh * LANES, (hh + 1) * LANES)
        q, k, v = q_ref[:, sl], k_ref[:, sl], v_ref[:, sl]
        S = s_sc[hh]
        gcol = jnp.sum(jnp.where(lane == head, gc_all, 0.0), axis=1, keepdims=True)
        bcol = jnp.sum(jnp.where(lane == head, beta_all, 0.0), axis=1, keepdims=True)
        grow = jnp.sum(jnp.where(sub == head, gr_all, 0.0), axis=0, keepdims=True)
        decay = jnp.exp(jnp.where(incl, gcol - grow, -jnp.inf))
        qb, kb, Sb = q.astype(BF16), k.astype(BF16), S.astype(BF16)
        kk = _dot_nt(kb, kb)
        L = jnp.where(strict, bcol * decay * kk, 0.0)
        Tinv = _unit_lower_inverse(L, C, eye, blockdiag)
        eG = jnp.exp(gcol)
        rhs = bcol * (v - eG * _dot(kb, Sb))
        U = _dot3(_split_bf16(Tinv), _split_bf16(rhs))
        Ub = U.astype(BF16)
        qk = _dot_nt(qb, kb) * decay
        o = eG * _dot(qb, Sb) + _dot(qk.astype(BF16), Ub)
        g_last = gcol[C - 1:C, :]
        s_sc[hh] = jnp.exp(g_last) * S + _dot_tn((k * jnp.exp(g_last - gcol)).astype(BF16), Ub)
        ms = jnp.mean(o * o, axis=-1, keepdims=True)
        on = o * lax.rsqrt(ms + NORM_EPS) * on_ref[...]
        o_ref[:, sl] = (on * _silu(z_ref[:, sl].astype(F32))).astype(o_ref.dtype)

    @pl.when(c == pl.num_programs(2) - 1)
    def _():
        s_out_ref[...] = s_sc[...]


def gdn_scan(qkvn, y_small, z, A_log, dt_bias, o_norm, S0, *, s0_layer, n_seq, T, C, qkv_blk0, qkv_blk_stride,
             tok_row0, beta_col, alpha_col, hb=4):
    H = S0.shape[2]
    DK, DV = S0.shape[3], S0.shape[4]
    assert DK == LANES and DV == LANES and H <= LANES and H % hb == 0
    nc = T // C
    ng = H // hb
    tb0 = tok_row0 // C

    def pad_lanes(a):
        return jnp.zeros((1, LANES), F32).at[0, :H].set(a.astype(F32))

    def qmap(s, g, c, *, grp):
        return (qkv_blk0 + qkv_blk_stride * (s * nc + c), grp * ng + g)

    tokmap = lambda s, g, c, col: (tb0 + s * nc + c, col)
    const = lambda s, g, c: (0, 0)
    smap = lambda s, g, c: (s, g, 0, 0)
    return pl.pallas_call(
        functools.partial(_gdn_scan_kernel, hb=hb, C=C),
        grid=(n_seq, ng, nc),
        in_specs=[
            pl.BlockSpec((C, hb * LANES), functools.partial(qmap, grp=0)),
            pl.BlockSpec((C, hb * LANES), functools.partial(qmap, grp=1)),
            pl.BlockSpec((C, hb * LANES), functools.partial(qmap, grp=2)),
            pl.BlockSpec((C, LANES), functools.partial(tokmap, col=beta_col)),
            pl.BlockSpec((C, LANES), functools.partial(tokmap, col=alpha_col)),
            pl.BlockSpec((C, hb * LANES), lambda s, g, c: (tb0 + s * nc + c, g)),
            pl.BlockSpec((1, LANES), const), pl.BlockSpec((1, LANES), const), pl.BlockSpec((1, LANES), const),
            pl.BlockSpec((None, None, hb, DK, DV), lambda s, g, c: (s0_layer, s, g, 0, 0)),
        ],
        out_specs=[pl.BlockSpec((C, hb * LANES), lambda s, g, c: (s * nc + c, g)),
                   pl.BlockSpec((None, hb, DK, DV), smap)],
        out_shape=[jax.ShapeDtypeStruct((n_seq * T, H * DV), F32), jax.ShapeDtypeStruct((n_seq, H, DK, DV), F32)],
        scratch_shapes=[pltpu.VMEM((hb, DK, DV), F32)],
        compiler_params=_cparams("parallel", "parallel", "arbitrary"),
        name="gdn_scan",
    )(qkvn, qkvn, qkvn, y_small, y_small, z, pad_lanes(A_log), pad_lanes(dt_bias), o_norm.reshape(1, DV).astype(F32), S0)


CONF_ROW_CHUNK = 32
CONF_LANE_CHUNK = 512


def _conformer_kernel(x_ref, halo_ref, w_ref, b_ref, g_ref, beta_ref, o_ref, xs, ys, *, taps, tiles_per_seq):
    i = pl.program_id(0)
    halo_rows = halo_ref.shape[0]
    tt, width = x_ref.shape
    keep = (i % tiles_per_seq != 0).astype(F32)
    xs[0:halo_rows, :] = halo_ref[...] * keep
    xs[halo_rows:halo_rows + tt, :] = x_ref[...]
    base = halo_rows - (taps - 1)

    for lc in range(width // CONF_LANE_CHUNK):
        ls = slice(lc * CONF_LANE_CHUNK, (lc + 1) * CONF_LANE_CHUNK)

        def body(r, carry):
            r0 = pl.multiple_of(r * CONF_ROW_CHUNK, CONF_ROW_CHUNK)
            acc = jnp.zeros((CONF_ROW_CHUNK, CONF_LANE_CHUNK), F32)
            win = xs[pl.ds(r0, CONF_ROW_CHUNK + halo_rows), ls]
            for k in range(taps):
                acc = acc + w_ref[k:k + 1, ls] * win[base + k:base + k + CONF_ROW_CHUNK, :]
            ys[pl.ds(r0, CONF_ROW_CHUNK), ls] = acc
            return carry

        lax.fori_loop(0, tt // CONF_ROW_CHUNK, body, 0)

    y = ys[...] + b_ref[...]
    mu = jnp.mean(y, axis=-1, keepdims=True)
    yc = y - mu
    var = jnp.mean(yc * yc, axis=-1, keepdims=True)
    h = yc * lax.rsqrt(var + NORM_EPS) * g_ref[...] + beta_ref[...]
    o_ref[...] = _silu(h).astype(o_ref.dtype)


def conformer_conv(u, dw_w, dw_b, ln_g, ln_b, *, row0, rows, tiles_per_seq, tt):
    taps, width = dw_w.shape
    halo = 32
    assert taps - 1 <= halo and row0 % tt == 0 and rows % tt == 0 and tt % halo == 0
    b0, hb0 = row0 // tt, row0 // halo
    vec = lambda a: a.reshape(1, width).astype(F32)
    const = lambda i: (0, 0)
    return pl.pallas_call(
        functools.partial(_conformer_kernel, taps=taps, tiles_per_seq=tiles_per_seq),
        grid=(rows // tt,),
        in_specs=[
            pl.BlockSpec((tt, width), lambda i: (b0 + i, 0)),
            pl.BlockSpec((halo, width), lambda i: (jnp.maximum(hb0 + i * (tt // halo) - 1, 0), 0)),
            pl.BlockSpec((taps, width), const),
            pl.BlockSpec((1, width), const), pl.BlockSpec((1, width), const), pl.BlockSpec((1, width), const),
        ],
        out_specs=pl.BlockSpec((tt, width), lambda i: (i, 0)),
        out_shape=jax.ShapeDtypeStruct((rows, width), BF16),
        scratch_shapes=[pltpu.VMEM((tt + halo, width), F32), pltpu.VMEM((tt, width), F32)],
        compiler_params=_cparams("parallel"),
        name="conformer_conv",
    )(u, u, dw_w, vec(dw_b), vec(ln_g), vec(ln_b))


def _router_kernel(x_ref, g_ref, w_ref, o_ref, *, n_experts):
    x = x_ref[...]
    ms = jnp.mean(x * x, axis=-1, keepdims=True)
    xn = x * lax.rsqrt(ms + NORM_EPS) * g_ref[...]
    logits = _dot(xn, w_ref[...], HIGHEST)
    lane = lax.broadcasted_iota(jnp.int32, logits.shape, 1)
    logits = jnp.where(lane < n_experts, logits, -jnp.inf)
    e = jnp.exp(logits - jnp.max(logits, axis=-1, keepdims=True))
    probs = e / jnp.sum(e, axis=-1, keepdims=True)
    p1 = jnp.max(probs, axis=-1, keepdims=True)
    i1 = jnp.min(jnp.where(probs == p1, lane, LANES), axis=-1, keepdims=True)
    rest = jnp.where(lane == i1, -1.0, probs)
    p2 = jnp.max(rest, axis=-1, keepdims=True)
    i2 = jnp.min(jnp.where(rest == p2, lane, LANES), axis=-1, keepdims=True)
    denom = p1 + p2
    o_ref[...] = jnp.where(lane == i1, p1 / denom, 0.0) + jnp.where(lane == i2, p2 / denom, 0.0)


def moe_gates(x, norm_g, router_w, *, tm=256):
    M, D = x.shape
    n_experts = router_w.shape[1]
    w_pad = jnp.zeros((D, LANES), F32).at[:, :n_experts].set(router_w)
    return pl.pallas_call(
        functools.partial(_router_kernel, n_experts=n_experts),
        grid=(M // tm,),
        in_specs=[pl.BlockSpec((tm, D), lambda i: (i, 0)), pl.BlockSpec((1, D), lambda i: (0, 0)),
                  pl.BlockSpec((D, LANES), lambda i: (0, 0))],
        out_specs=pl.BlockSpec((tm, LANES), lambda i: (i, 0)),
        out_shape=jax.ShapeDtypeStruct((M, LANES), F32),
        compiler_params=_cparams("parallel"),
        name="moe_router",
    )(x, norm_g.reshape(1, D), w_pad)


def _swiglu_epilogue(dots, extras):
    return _silu(dots[0]) * dots[1]


def _rope_tables(positions, rope_d):
    half = rope_d // 2
    inv_freq = jnp.power(ROPE_THETA, -jnp.arange(half, dtype=F32) / half)
    ang = positions[:, None] * inv_freq[None, :]
    cos, sin = jnp.cos(ang), jnp.sin(ang)
    return jnp.concatenate([cos, cos], axis=-1), jnp.concatenate([-sin, sin], axis=-1)


def kernel(x_prompt, x_sample, cache_kv_latent, cache_k_rope, state_gdn, state_gdn_conv, state_conf_conv, page_table, attn_norm, w_in, q_norm, kv_norm, w_q_b, w_uk, w_uv, gdn_conv_w, gdn_A_log, gdn_dt_bias, gdn_o_norm, conf_dw_w, conf_dw_b, conf_ln_g, conf_ln_b, w_o_mla, w_o_gdn, w_o_conf, w_out, ffn_norm, ffn_w1, ffn_w3, ffn_w2, moe_router, moe_w1, moe_w3, moe_w2, final_norm):
    B, T, D = x_prompt.shape
    Bd, Td, _ = x_sample.shape
    depth = w_in.shape[0]
    q_lora = q_norm.shape[1]
    kv_lora = kv_norm.shape[1]
    H, nope = w_uk.shape[2], w_uk.shape[3]
    rope_d = w_q_b.shape[3] - nope
    v_head = w_uv.shape[3]
    gdn_qkv = gdn_conv_w.shape[2]
    gdn_val = gdn_qkv // 3
    GH = gdn_A_log.shape[1]
    conf_c = conf_dw_w.shape[2]
    conf_k = conf_dw_w.shape[1]
    gdn_k = gdn_conv_w.shape[1]
    n_pages = page_table.shape[1]
    past_len = n_pages * cache_kv_latent.shape[2]
    Mp, Ms = B * T, Bd * Td
    M = Mp + Ms
    TM = 1024
    assert M % TM == 0 and Td == 8 and q_lora % kv_lora == 0 and (q_lora + kv_lora) % (3 * LANES) == 0

    sizes = (q_lora, kv_lora, rope_d, gdn_qkv, gdn_val, GH, GH, 2 * conf_c, 3 * D)
    offs = [0]
    for s in sizes:
        offs.append(offs[-1] + s)
    o_cq, o_ckv, o_kpe, o_qkv, o_z, o_beta, o_alpha, o_glu, o_gate, _ = offs

    X = jnp.concatenate([x_prompt.reshape(Mp, D), x_sample.reshape(Ms, D)], axis=0)
    pos = jnp.concatenate([jnp.tile(jnp.arange(T, dtype=F32), B), jnp.tile(past_len + jnp.arange(Td, dtype=F32), Bd)])
    cosf, sinf = _rope_tables(pos, rope_d)
    mla_scale = (nope + rope_d) ** -0.5
    zero_S = jnp.zeros((1, B, GH, LANES, LANES), F32)
    cache_k_ropeT = jnp.swapaxes(cache_k_rope, 2, 3)

    def lane_pad(w, width):
        return jnp.pad(w, ((0, 0), (0, width - w.shape[1])))

    outs = {k: [] for k in ("kv_p", "kr_p", "S_p", "gc_p", "cc_p", "kv_s", "kr_s", "S_s", "gc_s", "cc_s")}
    small_w = q_lora + kv_lora + 3 * LANES
    beta_col = (q_lora + kv_lora) // LANES + 1
    alpha_col = beta_col + 1

    for l in range(depth):
        wl = w_in[l]
        w_small = jnp.concatenate([
            wl[:, o_cq:o_kpe], lane_pad(wl[:, o_kpe:o_qkv], LANES), lane_pad(wl[:, o_beta:o_alpha], LANES),
            lane_pad(wl[:, o_alpha:o_glu], LANES)], axis=1).astype(BF16)
        w_qkv = wl[:, o_qkv:o_z].astype(BF16)
        w_z = wl[:, o_z:o_beta].astype(BF16)
        w_glu = wl[:, o_glu:o_gate].astype(BF16)
        w_gate = wl[:, o_gate:].astype(BF16)

        xn = rmsnorm(X, attn_norm[l], BF16)
        y_small = matmul(xn, w_small, F32, tm=TM, tn=small_w // 3, name="proj_small")
        qkv = matmul(xn, w_qkv, F32, tm=TM, tn=1024, name="proj_qkv")
        z = matmul(xn, w_z, F32, tm=TM, tn=1024, name="proj_z")
        tn_glu = 512
        u = matmul(xn, w_glu, F32, tm=TM, tn=tn_glu, n_out=conf_c, w_col_blocks=(0, conf_c // tn_glu),
                   epilogue=lambda dots, ex: dots[0] * _sigmoid(dots[1]), name="proj_glu")
        gates = matmul(xn, w_gate, BF16, tm=TM, tn=1024, epilogue=lambda dots, ex: _sigmoid(dots[0]), name="proj_gates")

        wq = w_q_b[l].astype(BF16)
        wqn = jnp.transpose(wq[:, :, :nope], (1, 0, 2))
        wqp = jnp.transpose(wq[:, :, nope:], (1, 0, 2))
        wukT = jnp.transpose(w_uk[l].astype(BF16), (1, 2, 0))
        wuvT = jnp.transpose(w_uv[l].astype(BF16), (1, 0, 2))
        q_lat, q_pe = mla_q(y_small, q_norm[l], wqn, wqp, wukT, cosf, sinf, q_lora=q_lora, scale=mla_scale)
        ckv, kpe, ckv_bf, kpe_bf = mla_kv(y_small, kv_norm[l], cosf, sinf, q_lora=q_lora, kv_lora=kv_lora)
        o_lat = mla_prompt_attention(q_lat, q_pe, ckv_bf, kpe_bf, B=B, T=T, m_total=M)

        def to_sample_rows(a):
            return jnp.transpose(a[:, Mp:, :].reshape(H, Bd, Td, -1), (1, 0, 2, 3)).reshape(Bd, H * Td, -1)

        o_lat_s = mla_sample_attention(to_sample_rows(q_lat), to_sample_rows(q_pe), cache_kv_latent, cache_k_ropeT,
                                       page_table, ckv, kpe, layer=l, new_row0=Mp)
        o_lat_s = jnp.transpose(o_lat_s.reshape(Bd, H, Td, kv_lora), (1, 0, 2, 3)).reshape(H, Ms, kv_lora)
        o_lat = lax.dynamic_update_slice(o_lat, o_lat_s, (0, Mp, 0))
        o_mla = fused_matmul(
            [(o_lat, pl.BlockSpec((None, TM, kv_lora), lambda i, h, k: (h, i, 0)))],
            [(wuvT, pl.BlockSpec((None, kv_lora, v_head), lambda i, h, k: (h, 0, 0)))],
            [0], [], lambda dots, ex: dots[0], jax.ShapeDtypeStruct((M, H * v_head), BF16),
            pl.BlockSpec((TM, v_head), lambda i, h, k: (i, h)), (M // TM, H, 1), None, "mla_uv")

        gconv = gdn_conv_w[l]
        grp = 2 * Td
        qkv_s = qkv[Mp:].reshape(Bd, Td, gdn_qkv)
        qkv_s_pad = jnp.concatenate([jnp.zeros((Bd, grp - Td - (gdn_k - 1), gdn_qkv), F32), state_gdn_conv[l], qkv_s],
                                    axis=1).reshape(Bd * grp, gdn_qkv)
        qkvn_p = gdn_prep(qkv, gconv, row0=0, rows=Mp, tiles_per_seq=T // 256, tt=256, key_width=gdn_val)
        qkvn_s = gdn_prep(qkv_s_pad, gconv, row0=0, rows=Bd * grp, tiles_per_seq=Bd * grp, tt=256, key_width=gdn_val)
        gdn_args = (gdn_A_log[l], gdn_dt_bias[l], gdn_o_norm[l])
        o_gdn_p, S_p = gdn_scan(qkvn_p, y_small, z, *gdn_args, zero_S, s0_layer=0, n_seq=B, T=T, C=math.gcd(T, 64),
                                qkv_blk0=0, qkv_blk_stride=1, tok_row0=0, beta_col=beta_col, alpha_col=alpha_col)
        o_gdn_s, S_s = gdn_scan(qkvn_s, y_small, z, *gdn_args, state_gdn, s0_layer=l, n_seq=Bd, T=Td,
                                C=math.gcd(Td, 64), qkv_blk0=1, qkv_blk_stride=2, tok_row0=Mp, beta_col=beta_col,
                                alpha_col=alpha_col)
        o_gdn = jnp.concatenate([o_gdn_p, o_gdn_s], axis=0).astype(BF16)

        cgrp = 40
        u_s = u[Mp:].reshape(Bd, Td, conf_c)
        u_s_pad = jnp.concatenate([jnp.zeros((Bd, cgrp - Td - (conf_k - 1), conf_c), F32), state_conf_conv[l], u_s],
                                  axis=1).reshape(Bd * cgrp, conf_c)
        conf_args = (conf_dw_w[l], conf_dw_b[l], conf_ln_g[l], conf_ln_b[l])
        h_conf_p = conformer_conv(u, *conf_args, row0=0, rows=Mp, tiles_per_seq=T // 256, tt=256)
        h_conf_s = conformer_conv(u_s_pad, *conf_args, row0=0, rows=Bd * cgrp, tiles_per_seq=Bd * cgrp, tt=320)
        h_conf = jnp.concatenate([h_conf_p, h_conf_s.reshape(Bd, cgrp, conf_c)[:, cgrp - Td:].reshape(Ms, conf_c)], axis=0)

        tn_m = 512
        kspec = lambda width: pl.BlockSpec((TM, width), lambda i, j, k: (i, 0))
        wspec = lambda width: pl.BlockSpec((width, tn_m), lambda i, j, k: (0, j))
        merged = fused_matmul(
            [(o_mla, kspec(H * v_head)), (o_gdn, kspec(gdn_val)), (h_conf, kspec(conf_c))],
            [(w_o_mla[l].astype(BF16), wspec(H * v_head)), (w_o_gdn[l].astype(BF16), wspec(gdn_val)),
             (w_o_conf[l].astype(BF16), wspec(conf_c))],
            [0, 1, 2],
            [(gates, _tile_spec(TM, tn_m, br * (D // tn_m))) for br in range(3)],
            lambda dots, ex: sum(ex[br].astype(F32) * dots[br] for br in range(3)),
            jax.ShapeDtypeStruct((M, D), BF16), _tile_spec(TM, tn_m), (M // TM, D // tn_m, 1), None, "merge")
        res_ep = lambda dots, ex: ex[0] + dots[0]
        X = matmul(merged, w_out[l].astype(BF16), F32, tm=TM, tn=512, epilogue=res_ep,
                   extras=[(X, _tile_spec(TM, 512))], name="w_out")

        i = l // 2
        if l % 2 == 0:
            xn2 = rmsnorm(X, ffn_norm[l], BF16)
            d_ff = ffn_w1.shape[2]
            h = fused_matmul(
                [(xn2, pl.BlockSpec((TM, D), lambda i_, j, k: (i_, 0)))],
                [(ffn_w1[i].astype(BF16), pl.BlockSpec((D, 512), lambda i_, j, k: (0, j))),
                 (ffn_w3[i].astype(BF16), pl.BlockSpec((D, 512), lambda i_, j, k: (0, j)))],
                [0, 0], [], _swiglu_epilogue, jax.ShapeDtypeStruct((M, d_ff), BF16), _tile_spec(TM, 512),
                (M // TM, d_ff // 512, 1), None, "ffn_up")
            X = matmul(h, ffn_w2[i].astype(BF16), F32, tm=TM, tn=512, tk=4096, epilogue=res_ep,
                       extras=[(X, _tile_spec(TM, 512))], name="ffn_down")
        else:
            xn2 = rmsnorm(X, ffn_norm[l], BF16)
            gate = moe_gates(X, ffn_norm[l], moe_router[i])
            E, _, d_e = moe_w1.shape[1:]
            per_e = d_e // 512

            def gated_swiglu(dots, ex):
                e = pl.program_id(1) // per_e
                lane = lax.broadcasted_iota(jnp.int32, ex[0].shape, 1)
                g = jnp.sum(jnp.where(lane == e, ex[0], 0.0), axis=1, keepdims=True)
                return g * _silu(dots[0]) * dots[1]

            wmap = lambda i_, j, k: (j // per_e, 0, j % per_e)
            h = fused_matmul(
                [(xn2, pl.BlockSpec((TM, D), lambda i_, j, k: (i_, 0)))],
                [(moe_w1[i].astype(BF16), pl.BlockSpec((None, D, 512), wmap)),
                 (moe_w3[i].astype(BF16), pl.BlockSpec((None, D, 512), wmap))],
                [0, 0], [(gate, pl.BlockSpec((TM, LANES), lambda i_, j, k: (i_, 0)))], gated_swiglu,
                jax.ShapeDtypeStruct((M, E * d_e), BF16), _tile_spec(TM, 512), (M // TM, E * d_e // 512, 1), None, "moe_up")
            X = matmul(h, moe_w2[i].astype(BF16).reshape(E * d_e, D), F32, tm=TM, tn=512, tk=4096, epilogue=res_ep,
                       extras=[(X, _tile_spec(TM, 512))], name="moe_down")

        outs["kv_p"].append(ckv[:Mp].reshape(B, T, kv_lora))
        outs["kr_p"].append(kpe[:Mp].reshape(B, T, rope_d))
        outs["S_p"].append(S_p)
        outs["gc_p"].append(qkv[:Mp].reshape(B, T, gdn_qkv)[:, T - (gdn_k - 1):])
        outs["cc_p"].append(u[:Mp].reshape(B, T, conf_c)[:, T - (conf_k - 1):])
        outs["kv_s"].append(ckv[Mp:].reshape(Bd, Td, kv_lora))
        outs["kr_s"].append(kpe[Mp:].reshape(Bd, Td, rope_d))
        outs["S_s"].append(S_s)
        outs["gc_s"].append(jnp.concatenate([state_gdn_conv[l], qkv_s], axis=1)[:, Td:])
        outs["cc_s"].append(jnp.concatenate([state_conf_conv[l], u_s], axis=1)[:, Td:])

    y_p = rmsnorm(X, final_norm, F32, row0=0, rows=Mp).reshape(B, T, D)
    y_s = rmsnorm(X, final_norm, F32, row0=Mp, rows=Ms).reshape(Bd, Td, D)
    st = {k: jnp.stack(v) for k, v in outs.items()}
    return (y_p, y_s, st["kv_p"], st["kr_p"], st["S_p"], st["gc_p"], st["cc_p"],
            st["kv_s"], st["kr_s"], st["S_s"], st["gc_s"], st["cc_s"])
```

```python
import functools
import math

import jax
import jax.numpy as jnp
from jax import lax
from jax.experimental import pallas as pl
from jax.experimental.pallas import tpu as pltpu

F32 = jnp.float32
BF16 = jnp.bfloat16
HIGHEST = lax.Precision.HIGHEST
NORM_EPS = 1e-6
ROPE_THETA = 10000.0
VMEM_LIMIT_BYTES = 56 * 1024 * 1024
LANES = 128
TRI_BLOCK = 16
SAMPLE_PAGE_GROUP = 4


def _cparams(*sem):
    return pltpu.CompilerParams(dimension_semantics=sem, vmem_limit_bytes=VMEM_LIMIT_BYTES)


def _dot(a, b, precision=None):
    return jnp.dot(a, b, preferred_element_type=F32, precision=precision)


def _dot_nt(a, b, precision=None):
    return lax.dot_general(a, b, (((1,), (1,)), ((), ())), preferred_element_type=F32, precision=precision)


def _dot_tn(a, b, precision=None):
    return lax.dot_general(a, b, (((0,), (0,)), ((), ())), preferred_element_type=F32, precision=precision)


def _split_bf16(a):
    hi = a.astype(BF16)
    return hi, (a - hi.astype(F32)).astype(BF16)


def _dot3(a, b):
    return _dot(a[0], b[0]) + (_dot(a[0], b[1]) + _dot(a[1], b[0]))


def _sigmoid(x):
    return 1.0 / (1.0 + jnp.exp(-x))


def _silu(x):
    return x * _sigmoid(x)


def _rmsnorm_kernel(x_ref, g_ref, o_ref):
    x = x_ref[...]
    ms = jnp.mean(x * x, axis=-1, keepdims=True)
    o_ref[...] = (x * lax.rsqrt(ms + NORM_EPS) * g_ref[...]).astype(o_ref.dtype)


def rmsnorm(x, g, out_dtype, *, row0=0, rows=None, tm=256):
    M, D = x.shape
    rows = M - row0 if rows is None else rows
    tm = min(tm, rows)
    b0 = row0 // tm
    return pl.pallas_call(
        _rmsnorm_kernel,
        grid=(rows // tm,),
        in_specs=[pl.BlockSpec((tm, D), lambda i: (b0 + i, 0)), pl.BlockSpec((1, D), lambda i: (0, 0))],
        out_specs=pl.BlockSpec((tm, D), lambda i: (i, 0)),
        out_shape=jax.ShapeDtypeStruct((rows, D), out_dtype),
        compiler_params=_cparams("parallel"),
        name="rmsnorm",
    )(x, g.reshape(1, D))


def fused_matmul(a_ops, w_ops, amap, e_ops, epilogue, out_shape, out_spec, grid, acc_shape, name):
    na, nd, ne = len(a_ops), len(w_ops), len(e_ops)
    nk = grid[2]

    def kernel(*refs):
        a_refs = refs[:na]
        w_refs = refs[na:na + nd]
        e_refs = refs[na + nd:na + nd + ne]
        o_ref = refs[na + nd + ne]
        acc_refs = refs[na + nd + ne + 1:]
        dots = [_dot(a_refs[amap[d]][...], w_refs[d][...]) for d in range(nd)]
        if nk == 1:
            o_ref[...] = epilogue(dots, [e[...] for e in e_refs]).astype(o_ref.dtype)
        else:
            k = pl.program_id(2)

            @pl.when(k == 0)
            def _():
                for d in range(nd):
                    acc_refs[d][...] = dots[d]

            @pl.when(k > 0)
            def _():
                for d in range(nd):
                    acc_refs[d][...] += dots[d]

            @pl.when(k == nk - 1)
            def _():
                o_ref[...] = epilogue([acc[...] for acc in acc_refs], [e[...] for e in e_refs]).astype(o_ref.dtype)

    scratch = [pltpu.VMEM(acc_shape, F32) for _ in range(nd)] if nk > 1 else []
    ops = a_ops + w_ops + e_ops
    return pl.pallas_call(
        kernel,
        grid=grid,
        in_specs=[s for _, s in ops],
        out_specs=out_spec,
        out_shape=out_shape,
        scratch_shapes=scratch,
        compiler_params=_cparams("parallel", "parallel", "arbitrary"),
        name=name,
    )(*[a for a, _ in ops])


def matmul(a, w, out_dtype, *, tm, tn, tk=None, epilogue=None, extras=(), n_out=None, w_col_blocks=(0,), name="matmul"):
    M, K = a.shape
    n_out = w.shape[1] if n_out is None else n_out
    tk = K if tk is None else tk
    grid = (M // tm, n_out // tn, K // tk)
    a_ops = [(a, pl.BlockSpec((tm, tk), lambda i, j, k: (i, k)))]
    w_ops = [(w, pl.BlockSpec((tk, tn), functools.partial(lambda i, j, k, off: (k, j + off), off=off)))
             for off in w_col_blocks]
    if epilogue is None:
        epilogue = lambda dots, ex: dots[0]
    return fused_matmul(a_ops, w_ops, [0] * len(w_ops), list(extras), epilogue,
                        jax.ShapeDtypeStruct((M, n_out), out_dtype),
                        pl.BlockSpec((tm, tn), lambda i, j, k: (i, j)), grid, (tm, tn), name)


def _tile_spec(tm, tn, col_off=0):
    return pl.BlockSpec((tm, tn), lambda i, j, k: (i, j + col_off))


def _rope_rows(x, cosf, sinf):
    half = x.shape[-1] // 2
    swapped = jnp.concatenate([x[:, half:], x[:, :half]], axis=-1)
    return x * cosf + swapped * sinf


def _mla_q_kernel(cq_ref, qn_ref, wqn_ref, wqp_ref, wuk_ref, cos_ref, sin_ref, ql_ref, qp_ref, cqn_sc, *, scale):
    @pl.when(pl.program_id(1) == 0)
    def _():
        x = cq_ref[...]
        ms = jnp.mean(x * x, axis=-1, keepdims=True)
        cqn_sc[...] = (x * lax.rsqrt(ms + NORM_EPS) * qn_ref[...]).astype(BF16)

    cqn = cqn_sc[...]
    q_nope = _dot(cqn, wqn_ref[...])
    q_pe = _dot(cqn, wqp_ref[...])
    q_lat = _dot(q_nope.astype(BF16), wuk_ref[...])
    ql_ref[...] = (q_lat * scale).astype(BF16)
    qp_ref[...] = (_rope_rows(q_pe, cos_ref[...], sin_ref[...]) * scale).astype(BF16)


def mla_q(y_small, q_norm, wqn, wqp, wukT, cosf, sinf, *, q_lora, scale, tm=512):
    M = y_small.shape[0]
    H, _, nope = wqn.shape
    rope_d = wqp.shape[2]
    kv_lora = wukT.shape[2]
    return pl.pallas_call(
        functools.partial(_mla_q_kernel, scale=scale),
        grid=(M // tm, H),
        in_specs=[
            pl.BlockSpec((tm, q_lora), lambda i, h: (i, 0)),
            pl.BlockSpec((1, q_lora), lambda i, h: (0, 0)),
            pl.BlockSpec((None, q_lora, nope), lambda i, h: (h, 0, 0)),
            pl.BlockSpec((None, q_lora, rope_d), lambda i, h: (h, 0, 0)),
            pl.BlockSpec((None, nope, kv_lora), lambda i, h: (h, 0, 0)),
            pl.BlockSpec((tm, rope_d), lambda i, h: (i, 0)),
            pl.BlockSpec((tm, rope_d), lambda i, h: (i, 0)),
        ],
        out_specs=[pl.BlockSpec((None, tm, kv_lora), lambda i, h: (h, i, 0)),
                   pl.BlockSpec((None, tm, rope_d), lambda i, h: (h, i, 0))],
        out_shape=[jax.ShapeDtypeStruct((H, M, kv_lora), BF16), jax.ShapeDtypeStruct((H, M, rope_d), BF16)],
        scratch_shapes=[pltpu.VMEM((tm, q_lora), BF16)],
        compiler_params=_cparams("parallel", "arbitrary"),
        name="mla_q",
    )(y_small, q_norm.reshape(1, -1), wqn, wqp, wukT, cosf, sinf)


def _mla_kv_kernel(ckv_ref, kpe_ref, g_ref, cos_ref, sin_ref, ckv_o, kpe_o, ckvb_o, kpeb_o):
    x = ckv_ref[...]
    ms = jnp.mean(x * x, axis=-1, keepdims=True)
    c = x * lax.rsqrt(ms + NORM_EPS) * g_ref[...]
    rope_d = kpe_o.shape[-1]
    kr = _rope_rows(kpe_ref[...][:, :rope_d], cos_ref[...], sin_ref[...])
    ckv_o[...] = c
    kpe_o[...] = kr
    ckvb_o[...] = c.astype(BF16)
    kpeb_o[...] = kr.astype(BF16)


def mla_kv(y_small, kv_norm, cosf, sinf, *, q_lora, kv_lora, tm=512):
    M = y_small.shape[0]
    rope_d = cosf.shape[1]
    row = lambda i: (i, 0)
    return pl.pallas_call(
        _mla_kv_kernel,
        grid=(M // tm,),
        in_specs=[
            pl.BlockSpec((tm, kv_lora), lambda i: (i, q_lora // kv_lora)),
            pl.BlockSpec((tm, LANES), lambda i: (i, (q_lora + kv_lora) // LANES)),
            pl.BlockSpec((1, kv_lora), lambda i: (0, 0)),
            pl.BlockSpec((tm, rope_d), row),
            pl.BlockSpec((tm, rope_d), row),
        ],
        out_specs=[pl.BlockSpec((tm, kv_lora), row), pl.BlockSpec((tm, rope_d), row),
                   pl.BlockSpec((tm, kv_lora), row), pl.BlockSpec((tm, rope_d), row)],
        out_shape=[jax.ShapeDtypeStruct((M, kv_lora), F32), jax.ShapeDtypeStruct((M, rope_d), F32),
                   jax.ShapeDtypeStruct((M, kv_lora), BF16), jax.ShapeDtypeStruct((M, rope_d), BF16)],
        compiler_params=_cparams("parallel"),
        name="mla_kv",
    )(y_small, y_small, kv_norm.reshape(1, -1), cosf, sinf)


def _softmax_update(s, v_bf, m_sc, l_sc, acc_sc):
    m_prev = m_sc[...]
    m_new = jnp.maximum(m_prev, jnp.max(s, axis=-1, keepdims=True))
    p = jnp.exp(s - m_new)
    corr = jnp.exp(m_prev - m_new)
    l_sc[...] = l_sc[...] * corr + jnp.sum(p, axis=-1, keepdims=True)
    acc_sc[...] = acc_sc[...] * corr + _dot(p.astype(BF16), v_bf)
    m_sc[...] = m_new


def _prompt_attn_kernel(ql_ref, qp_ref, k_ref, kr_ref, o_ref, m_sc, l_sc, acc_sc, *, tq, tk):
    i = pl.program_id(1)
    j = pl.program_id(2)
    H = ql_ref.shape[0]
    R = H * tq
    j_last = (i * tq + tq - 1) // tk

    @pl.when(j == 0)
    def _():
        m_sc[...] = jnp.full(m_sc.shape, -jnp.inf, F32)
        l_sc[...] = jnp.zeros(l_sc.shape, F32)
        acc_sc[...] = jnp.zeros(acc_sc.shape, F32)

    @pl.when(j <= j_last)
    def _():
        ql = ql_ref[...].reshape(R, ql_ref.shape[-1])
        qp = qp_ref[...].reshape(R, qp_ref.shape[-1])
        k = k_ref[...]
        s = _dot_nt(ql, k) + _dot_nt(qp, kr_ref[...])
        q_pos = i * tq + lax.broadcasted_iota(jnp.int32, (R, tk), 0) % tq
        k_pos = j * tk + lax.broadcasted_iota(jnp.int32, (R, tk), 1)
        s = jnp.where(k_pos <= q_pos, s, -jnp.inf)
        _softmax_update(s, k, m_sc, l_sc, acc_sc)

    @pl.when(j == j_last)
    def _():
        o = acc_sc[...] / l_sc[...]
        o_ref[...] = o.astype(BF16).reshape(o_ref.shape)


def mla_prompt_attention(q_lat, q_pe, ckv_bf, kpe_bf, *, B, T, m_total, tq=128, tk=512):
    H, _, C = q_lat.shape
    rope_d = q_pe.shape[2]
    nq, nk = T // tq, T // tk

    def kmap(b, i, j):
        return (b * nk + jnp.minimum(j, (i * tq + tq - 1) // tk), 0)

    qmap = lambda b, i, j: (0, b * nq + i, 0)
    return pl.pallas_call(
        functools.partial(_prompt_attn_kernel, tq=tq, tk=tk),
        grid=(B, nq, nk),
        in_specs=[pl.BlockSpec((H, tq, C), qmap), pl.BlockSpec((H, tq, rope_d), qmap),
                  pl.BlockSpec((tk, C), kmap), pl.BlockSpec((tk, rope_d), kmap)],
        out_specs=pl.BlockSpec((H, tq, C), qmap),
        out_shape=jax.ShapeDtypeStruct((H, m_total, C), BF16),
        scratch_shapes=[pltpu.VMEM((H * tq, 1), F32), pltpu.VMEM((H * tq, 1), F32), pltpu.VMEM((H * tq, C), F32)],
        compiler_params=_cparams("parallel", "parallel", "arbitrary"),
        name="mla_prompt_attn",
    )(q_lat, q_pe, ckv_bf, kpe_bf)


def _sample_attn_kernel(pt_ref, ql_ref, qp_ref, *refs, pps, page, t_new):
    kv_refs = refs[:pps]
    kr_refs = refs[pps:2 * pps]
    nkv_ref, nkr_ref, o_ref, m_sc, l_sc, acc_sc = refs[2 * pps:]
    s_idx = pl.program_id(1)

    @pl.when(s_idx == 0)
    def _():
        m_sc[...] = jnp.full(m_sc.shape, -jnp.inf, F32)
        l_sc[...] = jnp.zeros(l_sc.shape, F32)
        acc_sc[...] = jnp.zeros(acc_sc.shape, F32)

    ql = ql_ref[...]
    qp = qp_ref[...]
    groups = range(pps // SAMPLE_PAGE_GROUP)
    members = lambda g: range(g * SAMPLE_PAGE_GROUP, (g + 1) * SAMPLE_PAGE_GROUP)
    kvb = [jnp.concatenate([kv_refs[j][...].astype(BF16) for j in members(g)], axis=0) for g in groups]
    krb = [jnp.concatenate([kr_refs[j][...].astype(BF16) for j in members(g)], axis=1) for g in groups]
    s = [_dot_nt(ql, kvb[g]) + _dot(qp, krb[g]) for g in groups]
    m_prev = m_sc[...]
    m_new = m_prev
    for g in groups:
        m_new = jnp.maximum(m_new, jnp.max(s[g], axis=-1, keepdims=True))
    p = [jnp.exp(s[g] - m_new) for g in groups]
    corr = jnp.exp(m_prev - m_new)
    l_sc[...] = l_sc[...] * corr + sum(jnp.sum(p[g], axis=-1, keepdims=True) for g in groups)
    acc_sc[...] = acc_sc[...] * corr + sum(_dot(p[g].astype(BF16), kvb[g]) for g in groups)
    m_sc[...] = m_new

    @pl.when(s_idx == pl.num_programs(1) - 1)
    def _():
        nk = nkv_ref[...].astype(BF16)
        R = ql.shape[0]
        s_new = _dot_nt(ql, nk) + _dot_nt(qp, nkr_ref[...].astype(BF16))
        q_t = lax.broadcasted_iota(jnp.int32, (R, t_new), 0) % t_new
        k_t = lax.broadcasted_iota(jnp.int32, (R, t_new), 1)
        s_new = jnp.where(k_t <= q_t, s_new, -jnp.inf)
        _softmax_update(s_new, nk, m_sc, l_sc, acc_sc)
        o_ref[...] = (acc_sc[...] / l_sc[...]).astype(BF16)


def mla_sample_attention(q_lat_s, q_pe_s, cache_kv, cache_krT, page_table, ckv_new, kpe_new, *, layer, new_row0, pps=16):
    Bd, R, C = q_lat_s.shape
    rope_d = q_pe_s.shape[2]
    n_pages = page_table.shape[1]
    page = cache_kv.shape[2]
    t_new = 8
    assert n_pages % pps == 0 and new_row0 % t_new == 0
    pt_flat = page_table.reshape(-1)

    def page_map(b, s, pt, *, j):
        return (layer, pt[b * n_pages + s * pps + j], 0, 0)

    qmap = lambda b, s, pt: (b, 0, 0)
    newmap = lambda b, s, pt: (new_row0 // t_new + b, 0)
    in_specs = [pl.BlockSpec((None, R, C), qmap), pl.BlockSpec((None, R, rope_d), qmap)]
    in_specs += [pl.BlockSpec((None, None, page, C), functools.partial(page_map, j=j)) for j in range(pps)]
    in_specs += [pl.BlockSpec((None, None, rope_d, page), functools.partial(page_map, j=j)) for j in range(pps)]
    in_specs += [pl.BlockSpec((t_new, C), newmap), pl.BlockSpec((t_new, rope_d), newmap)]
    grid_spec = pltpu.PrefetchScalarGridSpec(
        num_scalar_prefetch=1,
        grid=(Bd, n_pages // pps),
        in_specs=in_specs,
        out_specs=pl.BlockSpec((None, R, C), qmap),
        scratch_shapes=[pltpu.VMEM((R, 1), F32), pltpu.VMEM((R, 1), F32), pltpu.VMEM((R, C), F32)],
    )
    return pl.pallas_call(
        functools.partial(_sample_attn_kernel, pps=pps, page=page, t_new=t_new),
        grid_spec=grid_spec,
        out_shape=jax.ShapeDtypeStruct((Bd, R, C), BF16),
        compiler_params=_cparams("parallel", "arbitrary"),
        name="mla_sample_attn",
    )(pt_flat, q_lat_s, q_pe_s, *([cache_kv] * pps), *([cache_krT] * pps), ckv_new, kpe_new)


def _gdn_prep_kernel(x_ref, halo_ref, w_ref, o_ref, *, taps, tiles_per_seq, q_scale):
    i = pl.program_id(0)
    kind = pl.program_id(1)
    halo_rows = halo_ref.shape[0]
    tt = x_ref.shape[0]
    keep = (i % tiles_per_seq != 0).astype(F32)
    xx = jnp.concatenate([halo_ref[...] * keep, x_ref[...]], axis=0)
    w = w_ref[...]
    y = jnp.zeros(x_ref.shape, F32)
    for k in range(taps):
        off = halo_rows - (taps - 1) + k
        y = y + w[k:k + 1, :] * xx[off:off + tt, :]
    y = _silu(y)
    scale = jnp.where(kind == 0, q_scale, 1.0).astype(F32)
    for h in range(x_ref.shape[1] // LANES):
        yh = y[:, h * LANES:(h + 1) * LANES]
        nrm = yh * lax.rsqrt(jnp.sum(yh * yh, axis=-1, keepdims=True) + NORM_EPS) * scale
        o_ref[:, h * LANES:(h + 1) * LANES] = jnp.where(kind < 2, nrm, yh)


def gdn_prep(qkv, conv_w, *, row0, rows, tiles_per_seq, tt, key_width):
    taps, width = conv_w.shape
    halo = 8
    assert width == 3 * key_width and row0 % tt == 0 and rows % tt == 0
    b0, hb0 = row0 // tt, row0 // halo
    return pl.pallas_call(
        functools.partial(_gdn_prep_kernel, taps=taps, tiles_per_seq=tiles_per_seq, q_scale=LANES ** -0.5),
        grid=(rows // tt, 3),
        in_specs=[
            pl.BlockSpec((tt, key_width), lambda i, c: (b0 + i, c)),
            pl.BlockSpec((halo, key_width), lambda i, c: (jnp.maximum(hb0 + i * (tt // halo) - 1, 0), c)),
            pl.BlockSpec((taps, key_width), lambda i, c: (0, c)),
        ],
        out_specs=pl.BlockSpec((tt, key_width), lambda i, c: (i, c)),
        out_shape=jax.ShapeDtypeStruct((rows, width), F32),
        compiler_params=_cparams("parallel", "parallel"),
        name="gdn_prep",
    )(qkv, qkv, conv_w)


def _unit_lower_inverse(Ls, C, eye, blockdiag):
    n = range(len(Ls))
    blk = min(C, TRI_BLOCK)
    D = list(Ls) if C <= TRI_BLOCK else [jnp.where(blockdiag, L, 0.0) for L in Ls]
    P = [-d for d in D]
    T = [eye + p for p in P]
    for _ in range(int(math.log2(blk)) - 1):
        Ps = [_split_bf16(p) for p in P]
        P = [_dot3(ps, ps) for ps in Ps]
        Ts = [_split_bf16(t) for t in T]
        Ps = [_split_bf16(p) for p in P]
        T = [T[i] + _dot3(Ts[i], Ps[i]) for i in n]
    if C > TRI_BLOCK:
        assert C // TRI_BLOCK <= 4
        Ts = [_split_bf16(t) for t in T]
        Os = [_split_bf16(Ls[i] - D[i]) for i in n]
        Mx = [_dot3(Ts[i], Os[i]) for i in n]
        Ms = [_split_bf16(m) for m in Mx]
        M2s = [_split_bf16(_dot3(ms, ms)) for ms in Ms]
        ImM = [eye - m for m in Mx]
        ImMs = [_split_bf16(m) for m in ImM]
        R = [ImM[i] + _dot3(ImMs[i], M2s[i]) for i in n]
        Rs = [_split_bf16(r) for r in R]
        T = [_dot3(Rs[i], Ts[i]) for i in n]
    return T


def _gdn_scan_kernel(q_ref, k_ref, v_ref, br_ref, ar_ref, z_ref, alog_ref, dtb_ref, on_ref, s0_ref,
                     o_ref, s_out_ref, s_sc, *, hb, C):
    c = pl.program_id(2)
    hg = pl.program_id(1)

    @pl.when(c == 0)
    def _():
        s_sc[...] = s0_ref[...]

    row = lax.broadcasted_iota(jnp.int32, (C, C), 0)
    col = lax.broadcasted_iota(jnp.int32, (C, C), 1)
    incl = row >= col
    strict = row > col
    eye = (row == col).astype(F32)
    blockdiag = (row // TRI_BLOCK) == (col // TRI_BLOCK)

    beta_all = _sigmoid(br_ref[...])
    x = ar_ref[...] + dtb_ref[...]
    softplus = jnp.maximum(x, 0.0) + jnp.log(1.0 + jnp.exp(-jnp.abs(x)))
    g_all = -jnp.exp(alog_ref[...]) * softplus
    gc_all = _dot(incl.astype(F32), g_all, HIGHEST)
    gr_all = _dot_tn(g_all, (row <= col).astype(F32), HIGHEST)
    lane = lax.broadcasted_iota(jnp.int32, (C, LANES), 1)
    sub = lax.broadcasted_iota(jnp.int32, (LANES, C), 0)

    hs = range(hb)
    sls = [slice(hh * LANES, (hh + 1) * LANES) for hh in hs]
    k = [k_ref[:, sl] for sl in sls]
    S = [s_sc[hh] for hh in hs]
    gcol = [jnp.sum(jnp.where(lane == hg * hb + hh, gc_all, 0.0), axis=1, keepdims=True) for hh in hs]
    bcol = [jnp.sum(jnp.where(lane == hg * hb + hh, beta_all, 0.0), axis=1, keepdims=True) for hh in hs]
    grow = [jnp.sum(jnp.where(sub == hg * hb + hh, gr_all, 0.0), axis=0, keepdims=True) for hh in hs]
    decay = [jnp.exp(jnp.where(incl, gcol[h] - grow[h], -jnp.inf)) for h in hs]
    qb = [q_ref[:, sl].astype(BF16) for sl in sls]
    kb = [x.astype(BF16) for x in k]
    Sb = [x.astype(BF16) for x in S]
    kk = [_dot_nt(kb[h], kb[h]) for h in hs]
    kS = [_dot(kb[h], Sb[h]) for h in hs]
    qS = [_dot(qb[h], Sb[h]) for h in hs]
    qk = [_dot_nt(qb[h], kb[h]) * decay[h] for h in hs]
    L = [jnp.where(strict, bcol[h] * decay[h] * kk[h], 0.0) for h in hs]
    Tinv = _unit_lower_inverse(L, C, eye, blockdiag)
    eG = [jnp.exp(gcol[h]) for h in hs]
    rhs = [bcol[h] * (v_ref[:, sls[h]] - eG[h] * kS[h]) for h in hs]
    Ts = [_split_bf16(t) for t in Tinv]
    Rs = [_split_bf16(r) for r in rhs]
    Ub = [_dot3(Ts[h], Rs[h]).astype(BF16) for h in hs]
    o = [eG[h] * qS[h] + _dot(qk[h].astype(BF16), Ub[h]) for h in hs]
    g_last = [gcol[h][C - 1:C, :] for h in hs]
    kd = [(k[h] * jnp.exp(g_last[h] - gcol[h])).astype(BF16) for h in hs]
    dS = [_dot_tn(kd[h], Ub[h]) for h in hs]
    for h in hs:
        s_sc[h] = jnp.exp(g_last[h]) * S[h] + dS[h]
    for h in hs:
        ms = jnp.mean(o[h] * o[h], axis=-1, keepdims=True)
        on = o[h] * lax.rsqrt(ms + NORM_EPS) * on_ref[...]
        o_ref[:, sls[h]] = (on * _silu(z_ref[:, sls[h]].astype(F32))).astype(o_ref.dtype)

    @pl.when(c == pl.num_programs(2) - 1)
    def _():
        s_out_ref[...] = s_sc[...]


def gdn_scan(qkvn, y_small, z, A_log, dt_bias, o_norm, S0, *, s0_layer, n_seq, T, C, qkv_blk0, qkv_blk_stride,
             tok_row0, beta_col, alpha_col, hb=16):
    H = S0.shape[2]
    DK, DV = S0.shape[3], S0.shape[4]
    assert DK == LANES and DV == LANES and H <= LANES and H % hb == 0
    nc = T // C
    ng = H // hb
    tb0 = tok_row0 // C

    def pad_lanes(a):
        return jnp.zeros((1, LANES), F32).at[0, :H].set(a.astype(F32))

    def qmap(s, g, c, *, grp):
        return (qkv_blk0 + qkv_blk_stride * (s * nc + c), grp * ng + g)

    tokmap = lambda s, g, c, col: (tb0 + s * nc + c, col)
    const = lambda s, g, c: (0, 0)
    smap = lambda s, g, c: (s, g, 0, 0)
    return pl.pallas_call(
        functools.partial(_gdn_scan_kernel, hb=hb, C=C),
        grid=(n_seq, ng, nc),
        in_specs=[
            pl.BlockSpec((C, hb * LANES), functools.partial(qmap, grp=0)),
            pl.BlockSpec((C, hb * LANES), functools.partial(qmap, grp=1)),
            pl.BlockSpec((C, hb * LANES), functools.partial(qmap, grp=2)),
            pl.BlockSpec((C, LANES), functools.partial(tokmap, col=beta_col)),
            pl.BlockSpec((C, LANES), functools.partial(tokmap, col=alpha_col)),
            pl.BlockSpec((C, hb * LANES), lambda s, g, c: (tb0 + s * nc + c, g)),
            pl.BlockSpec((1, LANES), const), pl.BlockSpec((1, LANES), const), pl.BlockSpec((1, LANES), const),
            pl.BlockSpec((None, None, hb, DK, DV), lambda s, g, c: (s0_layer, s, g, 0, 0)),
        ],
        out_specs=[pl.BlockSpec((C, hb * LANES), lambda s, g, c: (s * nc + c, g)),
                   pl.BlockSpec((None, hb, DK, DV), smap)],
        out_shape=[jax.ShapeDtypeStruct((n_seq * T, H * DV), F32), jax.ShapeDtypeStruct((n_seq, H, DK, DV), F32)],
        scratch_shapes=[pltpu.VMEM((hb, DK, DV), F32)],
        compiler_params=_cparams("parallel", "parallel", "arbitrary"),
        name="gdn_scan",
    )(qkvn, qkvn, qkvn, y_small, y_small, z, pad_lanes(A_log), pad_lanes(dt_bias), o_norm.reshape(1, DV).astype(F32), S0)


CONF_ROW_CHUNK = 32
CONF_LANE_CHUNK = 512


def _conformer_kernel(x_ref, halo_ref, w_ref, b_ref, g_ref, beta_ref, o_ref, xs, ys, *, taps, tiles_per_seq):
    i = pl.program_id(0)
    halo_rows = halo_ref.shape[0]
    tt, width = x_ref.shape
    keep = (i % tiles_per_seq != 0).astype(F32)
    xs[0:halo_rows, :] = halo_ref[...] * keep
    xs[halo_rows:halo_rows + tt, :] = x_ref[...]
    base = halo_rows - (taps - 1)

    for lc in range(width // CONF_LANE_CHUNK):
        ls = slice(lc * CONF_LANE_CHUNK, (lc + 1) * CONF_LANE_CHUNK)

        def body(r, carry):
            r0 = pl.multiple_of(r * CONF_ROW_CHUNK, CONF_ROW_CHUNK)
            acc = jnp.zeros((CONF_ROW_CHUNK, CONF_LANE_CHUNK), F32)
            win = xs[pl.ds(r0, CONF_ROW_CHUNK + halo_rows), ls]
            for k in range(taps):
                acc = acc + w_ref[k:k + 1, ls] * win[base + k:base + k + CONF_ROW_CHUNK, :]
            ys[pl.ds(r0, CONF_ROW_CHUNK), ls] = acc
            return carry

        lax.fori_loop(0, tt // CONF_ROW_CHUNK, body, 0)

    y = ys[...] + b_ref[...]
    mu = jnp.mean(y, axis=-1, keepdims=True)
    yc = y - mu
    var = jnp.mean(yc * yc, axis=-1, keepdims=True)
    h = yc * lax.rsqrt(var + NORM_EPS) * g_ref[...] + beta_ref[...]
    o_ref[...] = _silu(h).astype(o_ref.dtype)


def conformer_conv(u, dw_w, dw_b, ln_g, ln_b, *, row0, rows, tiles_per_seq, tt):
    taps, width = dw_w.shape
    halo = 32
    assert taps - 1 <= halo and row0 % tt == 0 and rows % tt == 0 and tt % halo == 0
    b0, hb0 = row0 // tt, row0 // halo
    vec = lambda a: a.reshape(1, width).astype(F32)
    const = lambda i: (0, 0)
    return pl.pallas_call(
        functools.partial(_conformer_kernel, taps=taps, tiles_per_seq=tiles_per_seq),
        grid=(rows // tt,),
        in_specs=[
            pl.BlockSpec((tt, width), lambda i: (b0 + i, 0)),
            pl.BlockSpec((halo, width), lambda i: (jnp.maximum(hb0 + i * (tt // halo) - 1, 0), 0)),
            pl.BlockSpec((taps, width), const),
            pl.BlockSpec((1, width), const), pl.BlockSpec((1, width), const), pl.BlockSpec((1, width), const),
        ],
        out_specs=pl.BlockSpec((tt, width), lambda i: (i, 0)),
        out_shape=jax.ShapeDtypeStruct((rows, width), BF16),
        scratch_shapes=[pltpu.VMEM((tt + halo, width), F32), pltpu.VMEM((tt, width), F32)],
        compiler_params=_cparams("parallel"),
        name="conformer_conv",
    )(u, u, dw_w, vec(dw_b), vec(ln_g), vec(ln_b))


def _router_kernel(x_ref, g_ref, w_ref, xn_ref, idx_ref, gate_ref, *, n_experts):
    x = x_ref[...]
    ms = jnp.mean(x * x, axis=-1, keepdims=True)
    xn = x * lax.rsqrt(ms + NORM_EPS) * g_ref[...]
    logits = _dot(xn, w_ref[...], HIGHEST)
    lane = lax.broadcasted_iota(jnp.int32, logits.shape, 1)
    logits = jnp.where(lane < n_experts, logits, -jnp.inf)
    e = jnp.exp(logits - jnp.max(logits, axis=-1, keepdims=True))
    probs = e / jnp.sum(e, axis=-1, keepdims=True)
    p1 = jnp.max(probs, axis=-1, keepdims=True)
    i1 = jnp.min(jnp.where(probs == p1, lane, LANES), axis=-1, keepdims=True)
    rest = jnp.where(lane == i1, -1.0, probs)
    p2 = jnp.max(rest, axis=-1, keepdims=True)
    i2 = jnp.min(jnp.where(rest == p2, lane, LANES), axis=-1, keepdims=True)
    denom = p1 + p2
    xn_ref[...] = xn
    gate_ref[...] = jnp.where(lane == 0, p1 / denom, jnp.where(lane == 1, p2 / denom, 0.0))
    idx_ref[...] = jnp.where(lane == 0, i1, jnp.where(lane == 1, i2, 0))


def moe_route(x, norm_g, router_w, *, tm=256):
    M, D = x.shape
    n_experts = router_w.shape[1]
    w_pad = jnp.zeros((D, LANES), F32).at[:, :n_experts].set(router_w)
    row = lambda i: (i, 0)
    return pl.pallas_call(
        functools.partial(_router_kernel, n_experts=n_experts),
        grid=(M // tm,),
        in_specs=[pl.BlockSpec((tm, D), row), pl.BlockSpec((1, D), lambda i: (0, 0)),
                  pl.BlockSpec((D, LANES), lambda i: (0, 0))],
        out_specs=[pl.BlockSpec((tm, D), row), pl.BlockSpec((tm, LANES), row), pl.BlockSpec((tm, LANES), row)],
        out_shape=[jax.ShapeDtypeStruct((M, D), F32), jax.ShapeDtypeStruct((M, LANES), jnp.int32),
                   jax.ShapeDtypeStruct((M, LANES), F32)],
        compiler_params=_cparams("parallel"),
        name="moe_router",
    )(x, norm_g.reshape(1, D), w_pad)


MOE_TILE = 256


def _row_copy(src_hbm, dst, src_row, dst_row, sem):
    return pltpu.make_async_copy(src_hbm.at[pl.ds(src_row, 1)], dst.at[pl.ds(dst_row, 1)], sem)


def _moe_gather_kernel(tok_ref, x_hbm, o_ref, buf, sem):
    tm = buf.shape[0]
    base = pl.program_id(0) * tm

    def start(r, c):
        _row_copy(x_hbm, buf, tok_ref[base + r], r, sem).start()
        return c

    def wait(r, c):
        _row_copy(x_hbm, buf, 0, r, sem).wait()
        return c

    lax.fori_loop(0, tm, start, 0)
    lax.fori_loop(0, tm, wait, 0)
    o_ref[...] = buf[...].astype(o_ref.dtype)


def moe_gather(xn, src_tok, n_rows):
    M, D = xn.shape
    tm = MOE_TILE
    grid_spec = pltpu.PrefetchScalarGridSpec(
        num_scalar_prefetch=1, grid=(n_rows // tm,),
        in_specs=[pl.BlockSpec(memory_space=pl.ANY)],
        out_specs=pl.BlockSpec((tm, D), lambda i, tok: (i, 0)),
        scratch_shapes=[pltpu.VMEM((tm, D), F32), pltpu.SemaphoreType.DMA(())])
    return pl.pallas_call(
        _moe_gather_kernel, grid_spec=grid_spec, out_shape=jax.ShapeDtypeStruct((n_rows, D), BF16),
        compiler_params=_cparams("arbitrary"), name="moe_gather")(src_tok, xn)


def _moe_up_kernel(te_ref, nu_ref, a_ref, w1_ref, w3_ref, g_ref, o_ref):
    used = pl.program_id(1) < nu_ref[0]

    @pl.when(used)
    def _():
        a = a_ref[...]
        o_ref[...] = (g_ref[...] * _silu(_dot(a, w1_ref[...])) * _dot(a, w3_ref[...])).astype(o_ref.dtype)

    @pl.when(jnp.logical_not(used))
    def _():
        o_ref[...] = jnp.zeros(o_ref.shape, o_ref.dtype)


def _moe_down_kernel(te_ref, nu_ref, a_ref, w_ref, o_ref):
    used = pl.program_id(1) < nu_ref[0]

    @pl.when(used)
    def _():
        o_ref[...] = _dot(a_ref[...], w_ref[...])

    @pl.when(jnp.logical_not(used))
    def _():
        o_ref[...] = jnp.zeros(o_ref.shape, o_ref.dtype)


def moe_experts(xs, row_gate, tile_e, n_used, w1, w3, w2, *, tn=1024):
    R, D = xs.shape
    E, _, d_e = w1.shape
    tm = MOE_TILE
    nt = R // tm
    amap = lambda j, i, te, nu: (i, 0)
    up_spec = pltpu.PrefetchScalarGridSpec(
        num_scalar_prefetch=2, grid=(d_e // tn, nt),
        in_specs=[pl.BlockSpec((tm, D), amap),
                  pl.BlockSpec((None, D, tn), lambda j, i, te, nu: (te[i], 0, j)),
                  pl.BlockSpec((None, D, tn), lambda j, i, te, nu: (te[i], 0, j)),
                  pl.BlockSpec((tm, 1), amap)],
        out_specs=pl.BlockSpec((tm, tn), lambda j, i, te, nu: (i, j)))
    hs = pl.pallas_call(
        _moe_up_kernel, grid_spec=up_spec, out_shape=jax.ShapeDtypeStruct((R, d_e), BF16),
        compiler_params=_cparams("parallel", "arbitrary"), name="moe_up")(tile_e, n_used, xs, w1, w3, row_gate)
    down_spec = pltpu.PrefetchScalarGridSpec(
        num_scalar_prefetch=2, grid=(D // tn, nt),
        in_specs=[pl.BlockSpec((tm, d_e), amap),
                  pl.BlockSpec((None, d_e, tn), lambda j, i, te, nu: (te[i], 0, j))],
        out_specs=pl.BlockSpec((tm, tn), lambda j, i, te, nu: (i, j)))
    return pl.pallas_call(
        _moe_down_kernel, grid_spec=down_spec, out_shape=jax.ShapeDtypeStruct((R, D), F32),
        compiler_params=_cparams("parallel", "arbitrary"), name="moe_down")(tile_e, n_used, hs, w2)


def _moe_combine_kernel(pos_ref, x_ref, ys_hbm, o_ref, buf, sem):
    tm = x_ref.shape[0]
    base = pl.program_id(0) * tm

    def start(r, c):
        for k in range(2):
            _row_copy(ys_hbm, buf.at[k], pos_ref[2 * (base + r) + k], r, sem).start()
        return c

    def wait(r, c):
        for k in range(2):
            _row_copy(ys_hbm, buf.at[k], 0, r, sem).wait()
        return c

    lax.fori_loop(0, tm, start, 0)
    lax.fori_loop(0, tm, wait, 0)
    o_ref[...] = x_ref[...] + (buf[0] + buf[1])


def moe_combine(x, ys, pos, *, tm=256):
    M, D = x.shape
    grid_spec = pltpu.PrefetchScalarGridSpec(
        num_scalar_prefetch=1, grid=(M // tm,),
        in_specs=[pl.BlockSpec((tm, D), lambda i, p: (i, 0)), pl.BlockSpec(memory_space=pl.ANY)],
        out_specs=pl.BlockSpec((tm, D), lambda i, p: (i, 0)),
        scratch_shapes=[pltpu.VMEM((2, tm, D), F32), pltpu.SemaphoreType.DMA(())])
    return pl.pallas_call(
        _moe_combine_kernel, grid_spec=grid_spec, out_shape=jax.ShapeDtypeStruct((M, D), F32),
        compiler_params=_cparams("arbitrary"), name="moe_combine")(pos, x, ys)


def moe_layout(idx, gate, n_experts):
    M = idx.shape[0]
    tm = MOE_TILE
    n_slots = 2 * M
    R = n_slots + n_experts * tm
    e_flat = idx[:, :2].reshape(-1)
    g_flat = gate[:, :2].reshape(-1)
    order = jnp.argsort(e_flat, stable=True).astype(jnp.int32)
    e_sorted = e_flat[order]
    counts = jnp.sum((e_flat[:, None] == jnp.arange(n_experts, dtype=jnp.int32)[None, :]).astype(jnp.int32), axis=0)
    gstart = jnp.cumsum(counts) - counts
    ptiles = (counts + tm - 1) // tm
    tile_end = jnp.cumsum(ptiles)
    pstart = (tile_end - ptiles) * tm
    prow = (pstart[e_sorted] + jnp.arange(n_slots, dtype=jnp.int32) - gstart[e_sorted]).astype(jnp.int32)
    src_tok = jnp.zeros((R,), jnp.int32).at[prow].set(order // 2)
    row_gate = jnp.zeros((R,), F32).at[prow].set(g_flat[order]).reshape(R, 1)
    pos = jnp.zeros((n_slots,), jnp.int32).at[order].set(prow)
    tile_ids = jnp.arange(R // tm, dtype=jnp.int32)
    tile_e = jnp.minimum(jnp.sum((tile_ids[:, None] >= tile_end[None, :]).astype(jnp.int32), axis=1), n_experts - 1)
    n_used = tile_end[-1:].astype(jnp.int32)
    return src_tok, row_gate, pos, tile_e.astype(jnp.int32), n_used


def _swiglu_epilogue(dots, extras):
    return _silu(dots[0]) * dots[1]


def _rope_tables(positions, rope_d):
    half = rope_d // 2
    inv_freq = jnp.power(ROPE_THETA, -jnp.arange(half, dtype=F32) / half)
    ang = positions[:, None] * inv_freq[None, :]
    cos, sin = jnp.cos(ang), jnp.sin(ang)
    return jnp.concatenate([cos, cos], axis=-1), jnp.concatenate([-sin, sin], axis=-1)


def kernel(x_prompt, x_sample, cache_kv_latent, cache_k_rope, state_gdn, state_gdn_conv, state_conf_conv, page_table, attn_norm, w_in, q_norm, kv_norm, w_q_b, w_uk, w_uv, gdn_conv_w, gdn_A_log, gdn_dt_bias, gdn_o_norm, conf_dw_w, conf_dw_b, conf_ln_g, conf_ln_b, w_o_mla, w_o_gdn, w_o_conf, w_out, ffn_norm, ffn_w1, ffn_w3, ffn_w2, moe_router, moe_w1, moe_w3, moe_w2, final_norm):
    B, T, D = x_prompt.shape
    Bd, Td, _ = x_sample.shape
    depth = w_in.shape[0]
    q_lora = q_norm.shape[1]
    kv_lora = kv_norm.shape[1]
    H, nope = w_uk.shape[2], w_uk.shape[3]
    rope_d = w_q_b.shape[3] - nope
    v_head = w_uv.shape[3]
    gdn_qkv = gdn_conv_w.shape[2]
    gdn_val = gdn_qkv // 3
    GH = gdn_A_log.shape[1]
    conf_c = conf_dw_w.shape[2]
    conf_k = conf_dw_w.shape[1]
    gdn_k = gdn_conv_w.shape[1]
    n_pages = page_table.shape[1]
    past_len = n_pages * cache_kv_latent.shape[2]
    Mp, Ms = B * T, Bd * Td
    M = Mp + Ms
    TM = 1024
    assert M % TM == 0 and Td == 8 and q_lora % kv_lora == 0 and (q_lora + kv_lora) % (3 * LANES) == 0

    sizes = (q_lora, kv_lora, rope_d, gdn_qkv, gdn_val, GH, GH, 2 * conf_c, 3 * D)
    offs = [0]
    for s in sizes:
        offs.append(offs[-1] + s)
    o_cq, o_ckv, o_kpe, o_qkv, o_z, o_beta, o_alpha, o_glu, o_gate, _ = offs

    X = jnp.concatenate([x_prompt.reshape(Mp, D), x_sample.reshape(Ms, D)], axis=0)
    pos = jnp.concatenate([jnp.tile(jnp.arange(T, dtype=F32), B), jnp.tile(past_len + jnp.arange(Td, dtype=F32), Bd)])
    cosf, sinf = _rope_tables(pos, rope_d)
    mla_scale = (nope + rope_d) ** -0.5
    zero_S = jnp.zeros((1, B, GH, LANES, LANES), F32)
    cache_k_ropeT = jnp.swapaxes(cache_k_rope, 2, 3)

    def lane_pad(w, width):
        return jnp.pad(w, ((0, 0), (0, width - w.shape[1])))

    outs = {k: [] for k in ("kv_p", "kr_p", "S_p", "gc_p", "cc_p", "kv_s", "kr_s", "S_s", "gc_s", "cc_s")}
    small_w = q_lora + kv_lora + 3 * LANES
    beta_col = (q_lora + kv_lora) // LANES + 1
    alpha_col = beta_col + 1

    for l in range(depth):
        wl = w_in[l]
        w_small = jnp.concatenate([
            wl[:, o_cq:o_kpe], lane_pad(wl[:, o_kpe:o_qkv], LANES), lane_pad(wl[:, o_beta:o_alpha], LANES),
            lane_pad(wl[:, o_alpha:o_glu], LANES)], axis=1).astype(BF16)
        w_qkv = wl[:, o_qkv:o_z].astype(BF16)
        w_z = wl[:, o_z:o_beta].astype(BF16)
        w_glu = wl[:, o_glu:o_gate].astype(BF16)
        w_gate = wl[:, o_gate:].astype(BF16)

        xn = rmsnorm(X, attn_norm[l], BF16)
        y_small = matmul(xn, w_small, F32, tm=TM, tn=small_w // 3, name="proj_small")
        qkv = matmul(xn, w_qkv, F32, tm=TM, tn=1024, name="proj_qkv")
        z = matmul(xn, w_z, F32, tm=TM, tn=1024, name="proj_z")
        tn_glu = 512
        u = matmul(xn, w_glu, F32, tm=TM, tn=tn_glu, n_out=conf_c, w_col_blocks=(0, conf_c // tn_glu),
                   epilogue=lambda dots, ex: dots[0] * _sigmoid(dots[1]), name="proj_glu")
        gates = matmul(xn, w_gate, BF16, tm=TM, tn=1024, epilogue=lambda dots, ex: _sigmoid(dots[0]), name="proj_gates")

        wq = w_q_b[l].astype(BF16)
        wqn = jnp.transpose(wq[:, :, :nope], (1, 0, 2))
        wqp = jnp.transpose(wq[:, :, nope:], (1, 0, 2))
        wukT = jnp.transpose(w_uk[l].astype(BF16), (1, 2, 0))
        wuvT = jnp.transpose(w_uv[l].astype(BF16), (1, 0, 2))
        q_lat, q_pe = mla_q(y_small, q_norm[l], wqn, wqp, wukT, cosf, sinf, q_lora=q_lora, scale=mla_scale)
        ckv, kpe, ckv_bf, kpe_bf = mla_kv(y_small, kv_norm[l], cosf, sinf, q_lora=q_lora, kv_lora=kv_lora)
        o_lat_p = mla_prompt_attention(q_lat, q_pe, ckv_bf, kpe_bf, B=B, T=T, m_total=Mp)

        def to_sample_rows(a):
            return jnp.transpose(a[:, Mp:, :].reshape(H, Bd, Td, -1), (1, 0, 2, 3)).reshape(Bd, H * Td, -1)

        o_lat_s = mla_sample_attention(to_sample_rows(q_lat), to_sample_rows(q_pe), cache_kv_latent, cache_k_ropeT,
                                       page_table, ckv, kpe, layer=l, new_row0=Mp)
        o_lat_s = jnp.transpose(o_lat_s.reshape(Bd, H, Td, kv_lora), (1, 0, 2, 3)).reshape(H, Ms, kv_lora)

        def value_up(o_lat):
            rows = o_lat.shape[1]
            return fused_matmul(
                [(o_lat, pl.BlockSpec((None, TM, kv_lora), lambda i, h, k: (h, i, 0)))],
                [(wuvT, pl.BlockSpec((None, kv_lora, v_head), lambda i, h, k: (h, 0, 0)))],
                [0], [], lambda dots, ex: dots[0], jax.ShapeDtypeStruct((rows, H * v_head), BF16),
                pl.BlockSpec((TM, v_head), lambda i, h, k: (i, h)), (rows // TM, H, 1), None, "mla_uv")

        o_mla = jnp.concatenate([value_up(o_lat_p), value_up(o_lat_s)], axis=0)

        gconv = gdn_conv_w[l]
        grp = 2 * Td
        qkv_s = qkv[Mp:].reshape(Bd, Td, gdn_qkv)
        qkv_s_pad = jnp.concatenate([jnp.zeros((Bd, grp - Td - (gdn_k - 1), gdn_qkv), F32), state_gdn_conv[l], qkv_s],
                                    axis=1).reshape(Bd * grp, gdn_qkv)
        qkvn_p = gdn_prep(qkv, gconv, row0=0, rows=Mp, tiles_per_seq=T // 256, tt=256, key_width=gdn_val)
        qkvn_s = gdn_prep(qkv_s_pad, gconv, row0=0, rows=Bd * grp, tiles_per_seq=Bd * grp, tt=256, key_width=gdn_val)
        gdn_args = (gdn_A_log[l], gdn_dt_bias[l], gdn_o_norm[l])
        o_gdn_p, S_p = gdn_scan(qkvn_p, y_small, z, *gdn_args, zero_S, s0_layer=0, n_seq=B, T=T, C=math.gcd(T, 64),
                                qkv_blk0=0, qkv_blk_stride=1, tok_row0=0, beta_col=beta_col, alpha_col=alpha_col)
        o_gdn_s, S_s = gdn_scan(qkvn_s, y_small, z, *gdn_args, state_gdn, s0_layer=l, n_seq=Bd, T=Td,
                                C=math.gcd(Td, 64), qkv_blk0=1, qkv_blk_stride=2, tok_row0=Mp, beta_col=beta_col,
                                alpha_col=alpha_col)
        o_gdn = jnp.concatenate([o_gdn_p, o_gdn_s], axis=0).astype(BF16)

        cgrp = 40
        u_s = u[Mp:].reshape(Bd, Td, conf_c)
        u_s_pad = jnp.concatenate([jnp.zeros((Bd, cgrp - Td - (conf_k - 1), conf_c), F32), state_conf_conv[l], u_s],
                                  axis=1).reshape(Bd * cgrp, conf_c)
        conf_args = (conf_dw_w[l], conf_dw_b[l], conf_ln_g[l], conf_ln_b[l])
        h_conf_p = conformer_conv(u, *conf_args, row0=0, rows=Mp, tiles_per_seq=T // 256, tt=256)
        h_conf_s = conformer_conv(u_s_pad, *conf_args, row0=0, rows=Bd * cgrp, tiles_per_seq=Bd * cgrp, tt=320)
        h_conf = jnp.concatenate([h_conf_p, h_conf_s.reshape(Bd, cgrp, conf_c)[:, cgrp - Td:].reshape(Ms, conf_c)], axis=0)

        tn_m = 512
        kspec = lambda width: pl.BlockSpec((TM, width), lambda i, j, k: (i, 0))
        wspec = lambda width: pl.BlockSpec((width, tn_m), lambda i, j, k: (0, j))
        merged = fused_matmul(
            [(o_mla, kspec(H * v_head)), (o_gdn, kspec(gdn_val)), (h_conf, kspec(conf_c))],
            [(w_o_mla[l].astype(BF16), wspec(H * v_head)), (w_o_gdn[l].astype(BF16), wspec(gdn_val)),
             (w_o_conf[l].astype(BF16), wspec(conf_c))],
            [0, 1, 2],
            [(gates, _tile_spec(TM, tn_m, br * (D // tn_m))) for br in range(3)],
            lambda dots, ex: sum(ex[br].astype(F32) * dots[br] for br in range(3)),
            jax.ShapeDtypeStruct((M, D), BF16), _tile_spec(TM, tn_m), (M // TM, D // tn_m, 1), None, "merge")
        res_ep = lambda dots, ex: ex[0] + dots[0]
        X = matmul(merged, w_out[l].astype(BF16), F32, tm=TM, tn=512, epilogue=res_ep,
                   extras=[(X, _tile_spec(TM, 512))], name="w_out")

        i = l // 2
        if l % 2 == 0:
            xn2 = rmsnorm(X, ffn_norm[l], BF16)
            d_ff = ffn_w1.shape[2]
            h = fused_matmul(
                [(xn2, pl.BlockSpec((TM, D), lambda i_, j, k: (i_, 0)))],
                [(ffn_w1[i].astype(BF16), pl.BlockSpec((D, 512), lambda i_, j, k: (0, j))),
                 (ffn_w3[i].astype(BF16), pl.BlockSpec((D, 512), lambda i_, j, k: (0, j)))],
                [0, 0], [], _swiglu_epilogue, jax.ShapeDtypeStruct((M, d_ff), BF16), _tile_spec(TM, 512),
                (M // TM, d_ff // 512, 1), None, "ffn_up")
            X = matmul(h, ffn_w2[i].astype(BF16), F32, tm=TM, tn=512, tk=4096, epilogue=res_ep,
                       extras=[(X, _tile_spec(TM, 512))], name="ffn_down")
        else:
            E = moe_w1.shape[1]
            xn2, top_idx, top_gate = moe_route(X, ffn_norm[l], moe_router[i])
            src_tok, row_gate, pos, tile_e, n_used = moe_layout(top_idx, top_gate, E)
            xs = moe_gather(xn2, src_tok, src_tok.shape[0])
            ys = moe_experts(xs, row_gate, tile_e, n_used, moe_w1[i].astype(BF16), moe_w3[i].astype(BF16),
                             moe_w2[i].astype(BF16))
            X = moe_combine(X, ys, pos)

        outs["kv_p"].append(ckv[:Mp].reshape(B, T, kv_lora))
        outs["kr_p"].append(kpe[:Mp].reshape(B, T, rope_d))
        outs["S_p"].append(S_p)
        outs["gc_p"].append(qkv[:Mp].reshape(B, T, gdn_qkv)[:, T - (gdn_k - 1):])
        outs["cc_p"].append(u[:Mp].reshape(B, T, conf_c)[:, T - (conf_k - 1):])
        outs["kv_s"].append(ckv[Mp:].reshape(Bd, Td, kv_lora))
        outs["kr_s"].append(kpe[Mp:].reshape(Bd, Td, rope_d))
        outs["S_s"].append(S_s)
        outs["gc_s"].append(jnp.concatenate([state_gdn_conv[l], qkv_s], axis=1)[:, Td:])
        outs["cc_s"].append(jnp.concatenate([state_conf_conv[l], u_s], axis=1)[:, Td:])

    y_p = rmsnorm(X, final_norm, F32, row0=0, rows=Mp).reshape(B, T, D)
    y_s = rmsnorm(X, final_norm, F32, row0=Mp, rows=Ms).reshape(Bd, Td, D)
    st = {k: jnp.stack(v) for k, v in outs.items()}
    return (y_p, y_s, st["kv_p"], st["kr_p"], st["S_p"], st["gc_p"], st["cc_p"],
            st["kv_s"], st["kr_s"], st["S_s"], st["gc_s"], st["cc_s"])
```
